```python
import jax, jax.numpy as jnp
from jax import lax
import numpy as np

D_MODEL = 1024
BATCH = 2
SEQ = 8192
DEPTH = 4
DEC_BATCH = 128
DEC_SEQ = 1
PAST_LEN = 2048
PAGE_SIZE = 128

N_MIXERS = 2
N_ATT = (DEPTH + N_MIXERS - 1) // N_MIXERS
N_POOL = DEPTH // N_MIXERS
N_HEADS = 16
HEAD_DIM = D_MODEL // N_HEADS
MOBA_BLOCK = 256
MOBA_TOPK = 3
Q_BLOCK = 128
POOL_WINDOWS = (2, 4, 8, 16)
N_POOL_GROUPS = len(POOL_WINDOWS)
POOL_GROUP = D_MODEL // N_POOL_GROUPS
POOL_HIST = max(POOL_WINDOWS) - 1
N_EXPERTS = 16
N_EXPERT_GROUPS = 4
EXPERTS_PER_GROUP = N_EXPERTS // N_EXPERT_GROUPS
EXPERT_TOPK = 2
D_EXPERT = D_MODEL // 2
MOE_BLOCK = 128
DEEPNORM_ALPHA = (2.0 * DEPTH) ** 0.25
DEEPNORM_BETA = (8.0 * DEPTH) ** -0.25
LN_EPS = 1e-5

kernel_name = 'moba_pool_hybrid_decoder_step'


def layer_norm(x, g, b):
    xf = x.astype(jnp.float32)
    mu = xf.mean(-1, keepdims=True)
    var = jnp.mean(jnp.square(xf - mu), -1, keepdims=True)
    return ((xf - mu) * lax.rsqrt(var + LN_EPS) * g + b).astype(x.dtype)


def adaln(c, w, b):
    mod = jax.nn.silu(c) @ w + b
    return jnp.split(mod[:, None, :], 6, axis=-1)


def alibi_slopes():
    return 2.0 ** (-(8.0 / N_HEADS) * jnp.arange(1, N_HEADS + 1, dtype=jnp.float32))


def gather_pages(pool, page_table):
    pages = pool[page_table]
    return pages.reshape(page_table.shape[0], -1, N_HEADS, HEAD_DIM)


def moba_query_block(q, pos, kblk, vblk, kmean, slopes):
    B, H, Q, _ = q.shape
    NB = kblk.shape[2]
    qf = q.astype(jnp.float32)
    own = pos // MOBA_BLOCK
    gate = jnp.einsum('bhqd,bhnd->bhqn', qf, kmean)
    fully_past = jnp.arange(NB)[None, :] < own[:, None]
    gate = jnp.where(fully_past, gate, -jnp.inf)
    if NB < MOBA_TOPK:
        gate = jnp.pad(gate, ((0, 0), (0, 0), (0, 0), (0, MOBA_TOPK - NB)), constant_values=-jnp.inf)
    _, sel = lax.top_k(gate, MOBA_TOPK)
    sel = jnp.minimum(sel, NB - 1)
    blocks = jnp.concatenate([sel, jnp.broadcast_to(own[None, None, :, None], (B, H, Q, 1))], -1)
    slot_valid = jnp.concatenate([jnp.arange(MOBA_TOPK)[None, :] < own[:, None],
                                  jnp.ones((Q, 1), bool)], -1)
    bi = jnp.arange(B)[:, None, None]
    hi = jnp.arange(H)[None, :, None]
    offs = jnp.arange(MOBA_BLOCK)
    scale = HEAD_DIM ** -0.5
    n_slots = MOBA_TOPK + 1
    scores = []
    for s in range(n_slots):
        blk = blocks[..., s]
        k_g = kblk[bi, hi, blk].astype(jnp.float32)
        key_pos = blk[..., None] * MOBA_BLOCK + offs
        dist = (pos[:, None] - key_pos).astype(jnp.float32)
        sc = jnp.einsum('bhqd,bhqkd->bhqk', qf, k_g) * scale - slopes[:, None, None] * dist
        allowed = slot_valid[:, s][:, None] & (dist >= 0)
        scores.append(jnp.where(allowed, sc, -jnp.inf))
    probs = jax.nn.softmax(jnp.concatenate(scores, -1), axis=-1)
    out = jnp.zeros((B, H, Q, HEAD_DIM), jnp.float32)
    for s in range(n_slots):
        v_g = vblk[bi, hi, blocks[..., s]].astype(jnp.float32)
        out = out + jnp.einsum('bhqk,bhqkd->bhqd', probs[..., s * MOBA_BLOCK:(s + 1) * MOBA_BLOCK], v_g)
    return out


def moba_attention(h, w_qkv, w_o, past_k, past_v, pos0):
    B, T, _ = h.shape
    qkv = (h @ w_qkv).reshape(B, T, 3, N_HEADS, HEAD_DIM)
    q, k, v = qkv[:, :, 0], qkv[:, :, 1], qkv[:, :, 2]
    k_all = k if past_k is None else jnp.concatenate([past_k.astype(k.dtype), k], 1)
    v_all = v if past_v is None else jnp.concatenate([past_v.astype(v.dtype), v], 1)
    L = k_all.shape[1]
    NB = -(-L // MOBA_BLOCK)
    pad = NB * MOBA_BLOCK - L

    def to_blocks(t):
        t = jnp.pad(t, ((0, 0), (0, pad), (0, 0), (0, 0)))
        return t.reshape(B, NB, MOBA_BLOCK, N_HEADS, HEAD_DIM).transpose(0, 3, 1, 2, 4)

    kblk, vblk = to_blocks(k_all), to_blocks(v_all)
    kmean = kblk.astype(jnp.float32).mean(3)
    slopes = alibi_slopes()
    qs = min(T, Q_BLOCK)
    nq = T // qs
    q_blocks = q.reshape(B, nq, qs, N_HEADS, HEAD_DIM).transpose(1, 0, 3, 2, 4)
    pos = (pos0 + jnp.arange(T, dtype=jnp.int32)).reshape(nq, qs)
    o = lax.map(lambda a: moba_query_block(a[0], a[1], kblk, vblk, kmean, slopes), (q_blocks, pos))
    o = o.transpose(1, 0, 3, 2, 4).reshape(B, T, D_MODEL).astype(h.dtype)
    return o @ w_o, k, v


def pool_mixer(h, hist, pos0, w_pool, pool_scale):
    B, T, D = h.shape
    P = POOL_HIST
    ext = jnp.concatenate([hist.astype(h.dtype), h], 1)
    cs0 = jnp.concatenate([jnp.zeros((B, 1, D), jnp.float32),
                           jnp.cumsum(ext.astype(jnp.float32), axis=1)], 1)
    top = cs0[:, P + 1:]
    pos = pos0 + jnp.arange(T, dtype=jnp.int32)
    parts = []
    for g, w in enumerate(POOL_WINDOWS):
        sl = slice(g * POOL_GROUP, (g + 1) * POOL_GROUP)
        lo = cs0[:, P + 1 - w:P + 1 - w + T, sl]
        cnt = jnp.minimum(pos + 1, w).astype(jnp.float32)[None, :, None]
        parts.append((top[..., sl] - lo) / cnt)
    mixed = jnp.concatenate(parts, -1) - h.astype(jnp.float32)
    y = jnp.einsum('btgc,gce->btge', mixed.reshape(B, T, N_POOL_GROUPS, POOL_GROUP),
                   w_pool.astype(jnp.float32)).reshape(B, T, D) * pool_scale
    return y.astype(h.dtype), ext[:, -P:]


def grouped_moe(h, w_router, b_router, w_gate, w_up, w_down):
    B, T, D = h.shape
    N = B * T
    xf = h.reshape(N, D)
    logits = xf.astype(jnp.float32) @ w_router.astype(jnp.float32) + b_router.astype(jnp.float32)
    aff = jax.nn.sigmoid(logits)
    grp = aff.reshape(N, N_EXPERT_GROUPS, EXPERTS_PER_GROUP)
    gscore = lax.top_k(grp, EXPERT_TOPK)[0].sum(-1)
    g_sel = jnp.argmax(gscore, -1)
    in_grp = jnp.take_along_axis(grp, g_sel[:, None, None], axis=1)[:, 0]
    w_top, local = lax.top_k(in_grp, EXPERT_TOPK)
    eid = g_sel[:, None] * EXPERTS_PER_GROUP + local
    gates = w_top / w_top.sum(-1, keepdims=True)
    M = N * EXPERT_TOPK
    e_flat = eid.reshape(M)
    order = jnp.argsort(e_flat)
    e_s = e_flat[order]
    tok_s = order // EXPERT_TOPK
    g_s = gates.reshape(M)[order]
    counts = jnp.zeros((N_EXPERTS,), jnp.int32).at[e_flat].add(1)
    starts = jnp.cumsum(counts) - counts
    padded = (counts + MOE_BLOCK - 1) // MOE_BLOCK * MOE_BLOCK
    pad_ends = jnp.cumsum(padded)
    pad_starts = pad_ends - padded
    dest = pad_starts[e_s] + jnp.arange(M, dtype=jnp.int32) - starts[e_s]
    n_blk = -(-(M + N_EXPERTS * (MOE_BLOCK - 1)) // MOE_BLOCK)
    xbuf = jnp.zeros((n_blk * MOE_BLOCK, D), h.dtype).at[dest].set(xf[tok_s])
    blk_expert = jnp.minimum(jnp.searchsorted(pad_ends, jnp.arange(n_blk, dtype=jnp.int32) * MOE_BLOCK,
                                              side='right'), N_EXPERTS - 1)

    def expert_block(args):
        xb, e = args
        a = jax.nn.silu(xb @ w_gate[e]) * (xb @ w_up[e])
        return a @ w_down[e]

    ybuf = lax.map(expert_block, (xbuf.reshape(n_blk, MOE_BLOCK, D), blk_expert)).reshape(-1, D)
    y = jnp.zeros((N, D), jnp.float32).at[tok_s].add(ybuf[dest].astype(jnp.float32) * g_s[:, None])
    return y.reshape(B, T, D).astype(h.dtype)


def run_trunk(x, c, pos0, cache_k, cache_v, page_table, pool_state, prm):
    new_k, new_v, new_pool = [], [], []
    for i in range(DEPTH):
        sh1, sc1, g1, sh2, sc2, g2 = adaln(c, prm['w_ada'][i], prm['b_ada'][i])
        h = x * (1 + sc1) + sh1
        j = i // N_MIXERS
        if i % N_MIXERS == 0:
            if cache_k is None:
                pk, pv = None, None
            else:
                pk, pv = gather_pages(cache_k[j], page_table), gather_pages(cache_v[j], page_table)
            out, k_new, v_new = moba_attention(h, prm['w_qkv'][j], prm['w_o'][j], pk, pv, pos0)
            new_k.append(k_new)
            new_v.append(v_new)
        else:
            if pool_state is None:
                hist = jnp.zeros((x.shape[0], POOL_HIST, D_MODEL), x.dtype)
            else:
                hist = pool_state[j]
            out, hist_new = pool_mixer(h, hist, pos0, prm['w_pool'][j], prm['pool_scale'][j])
            new_pool.append(hist_new)
        x = layer_norm(DEEPNORM_ALPHA * x + g1 * out, prm['ln_g'][i, 0], prm['ln_b'][i, 0])
        h2 = x * (1 + sc2) + sh2
        ffn = grouped_moe(h2, prm['w_router'], prm['b_router'],
                          prm['w_gate'][i], prm['w_up'][i], prm['w_down'][i])
        x = layer_norm(DEEPNORM_ALPHA * x + g2 * ffn, prm['ln_g'][i, 1], prm['ln_b'][i, 1])
    return x, jnp.stack(new_k), jnp.stack(new_v), jnp.stack(new_pool)


def setup_inputs(seed: int = 0) -> dict:
    key = jax.random.key(seed)
    ks = jax.random.split(key, 21)
    f32 = jnp.float32
    D = D_MODEL
    n_pages = PAST_LEN // PAGE_SIZE
    n_used = DEC_BATCH * n_pages
    n_phys = n_used + max(1, n_used // 4)

    def nrm(k, shape, s):
        return jax.random.normal(k, shape, f32) * s

    page_table = jax.random.permutation(ks[7], n_phys)[:n_used].reshape(DEC_BATCH, n_pages).astype(jnp.int32)
    qkv_col_scale = jnp.concatenate([jnp.ones((2 * D,), f32), jnp.full((D,), DEEPNORM_BETA, f32)])
    return {
        'x_prompt': nrm(ks[0], (BATCH, SEQ, D), 1.0),
        'x_sample': nrm(ks[1], (DEC_BATCH, DEC_SEQ, D), 1.0),
        'cache_k': nrm(ks[2], (N_ATT, n_phys, PAGE_SIZE, N_HEADS, HEAD_DIM), 1.0),
        'cache_v': nrm(ks[3], (N_ATT, n_phys, PAGE_SIZE, N_HEADS, HEAD_DIM), DEEPNORM_BETA),
        'state_pool': nrm(ks[4], (N_POOL, DEC_BATCH, POOL_HIST, D), 1.0),
        'page_table': page_table,
        'c_prompt': nrm(ks[5], (BATCH, D), 1.0),
        'c_sample': nrm(ks[6], (DEC_BATCH, D), 1.0),
        'w_ada': nrm(ks[8], (DEPTH, D, 6 * D), 0.5 * D ** -0.5),
        'b_ada': nrm(ks[9], (DEPTH, 6 * D), 0.02),
        'ln_g': 1.0 + nrm(ks[10], (DEPTH, 2, D), 0.02),
        'ln_b': nrm(ks[11], (DEPTH, 2, D), 0.02),
        'w_qkv': nrm(ks[12], (N_ATT, D, 3 * D), D ** -0.5) * qkv_col_scale,
        'w_o': nrm(ks[13], (N_ATT, D, D), DEEPNORM_BETA * D ** -0.5),
        'w_pool': nrm(ks[14], (N_POOL, N_POOL_GROUPS, POOL_GROUP, POOL_GROUP), DEEPNORM_BETA * POOL_GROUP ** -0.5),
        'pool_scale': 1.0 + nrm(ks[15], (N_POOL, D), 0.05),
        'w_router': nrm(ks[16], (D, N_EXPERTS), D ** -0.5),
        'b_router': nrm(ks[17], (N_EXPERTS,), 0.01),
        'w_gate': nrm(ks[18], (DEPTH, N_EXPERTS, D, D_EXPERT), D ** -0.5),
        'w_up': nrm(ks[19], (DEPTH, N_EXPERTS, D, D_EXPERT), D ** -0.5),
        'w_down': nrm(ks[20], (DEPTH, N_EXPERTS, D_EXPERT, D), DEEPNORM_BETA * D_EXPERT ** -0.5),
    }


def reference(x_prompt, x_sample, cache_k, cache_v, state_pool, page_table, c_prompt, c_sample,
              w_ada, b_ada, ln_g, ln_b, w_qkv, w_o, w_pool, pool_scale, w_router, b_router,
              w_gate, w_up, w_down):
    prm = dict(w_ada=w_ada, b_ada=b_ada, ln_g=ln_g, ln_b=ln_b, w_qkv=w_qkv, w_o=w_o,
               w_pool=w_pool, pool_scale=pool_scale, w_router=w_router, b_router=b_router,
               w_gate=w_gate, w_up=w_up, w_down=w_down)
    y_prompt, k_prompt, v_prompt, pool_prompt = run_trunk(
        x_prompt, c_prompt, 0, None, None, None, None, prm)
    y_sample, k_sample, v_sample, pool_sample = run_trunk(
        x_sample, c_sample, PAST_LEN, cache_k, cache_v, page_table, state_pool, prm)
    return (y_prompt, y_sample, k_prompt, v_prompt, pool_prompt, k_sample, v_sample, pool_sample)
```

```python
import functools

import jax
import jax.numpy as jnp
from jax import lax
from jax.experimental import pallas as pl
from jax.experimental.pallas import tpu as pltpu

F32 = jnp.float32
BF16 = jnp.bfloat16
I32 = jnp.int32

N_HEADS = 16
HEAD_DIM = 64
MOBA_BLOCK = 256
MOBA_TOPK = 3
POOL_WINDOWS = (2, 4, 8, 16)
POOL_HIST = max(POOL_WINDOWS) - 1
POOL_HALO = 32
N_EXPERTS = 16
N_EXPERT_GROUPS = 4
EXPERTS_PER_GROUP = N_EXPERTS // N_EXPERT_GROUPS
LN_EPS = 1e-5
LANES = 128
HEADS_PER_LANE_BLOCK = LANES // HEAD_DIM
MOE_ROWS = 256
VMEM_LIMIT = 48 * 1024 * 1024
NEG_INF = float("-inf")


def _dot(a, b):
    return jnp.dot(a, b, preferred_element_type=F32)


def _dot_nt(a, b):
    return lax.dot_general(a, b, (((1,), (1,)), ((), ())), preferred_element_type=F32)


def _split_bf16(x):
    hi = x.astype(BF16)
    lo = (x - hi.astype(F32)).astype(BF16)
    return hi, lo


def _params(*sem):
    return pltpu.CompilerParams(dimension_semantics=sem, vmem_limit_bytes=VMEM_LIMIT)


def _adaln_kernel(c_ref, w_ref, b_ref, o_ref):
    c = c_ref[...]
    s_hi, s_lo = _split_bf16(c * jax.nn.sigmoid(c))
    w_hi, w_lo = _split_bf16(w_ref[...])
    o_ref[...] = _dot(s_hi, w_hi) + _dot(s_lo, w_hi) + _dot(s_hi, w_lo) + b_ref[...]


def _adaln(c, w_ada, b_ada):
    depth, d, d6 = w_ada.shape
    bc = c.shape[0]
    tn = 1024
    return pl.pallas_call(
        _adaln_kernel,
        grid=(depth, d6 // tn),
        in_specs=[
            pl.BlockSpec((bc, d), lambda l, j: (0, 0)),
            pl.BlockSpec((None, d, tn), lambda l, j: (l, 0, j)),
            pl.BlockSpec((None, 1, tn), lambda l, j: (l, 0, j)),
        ],
        out_specs=pl.BlockSpec((None, bc, tn), lambda l, j: (l, 0, j)),
        out_shape=jax.ShapeDtypeStruct((depth, bc, d6), F32),
        compiler_params=_params("arbitrary", "arbitrary"),
    )(c, w_ada, b_ada.reshape(depth, 1, d6))


def _mod_spec(mod, layer, which, tm):
    r = mod.shape[3]
    d = mod.shape[4]
    if r == 1:
        return pl.BlockSpec((None, None, None, 1, d), lambda b, t: (layer, which, b, 0, 0))
    return pl.BlockSpec((None, None, None, tm, d), lambda b, t: (layer, which, b, t, 0))


def _qkv_kernel(x_ref, sh_ref, sc_ref, w_ref, q_ref, k_ref, v_ref, kb_ref, vb_ref, *km_ref, d, n_kblk):
    h = x_ref[...] * (1.0 + sc_ref[...]) + sh_ref[...]
    hb = h.astype(BF16)
    q = _dot(hb, w_ref[:, 0:d])
    k = _dot(hb, w_ref[:, d:2 * d])
    v = _dot(hb, w_ref[:, 2 * d:3 * d])
    q_ref[...] = q.astype(BF16)
    k_ref[...] = k
    v_ref[...] = v
    kb_ref[...] = k.astype(BF16)
    vb_ref[...] = v.astype(BF16)
    for i in range(n_kblk):
        km_ref[0][i] = jnp.mean(k[i * MOBA_BLOCK:(i + 1) * MOBA_BLOCK], axis=0, keepdims=True)


def _qkv(x, mod, layer, w_qkv_b, tm, with_kmean):
    b, t, d = x.shape
    n_kblk = tm // MOBA_BLOCK if with_kmean else 0
    tok = pl.BlockSpec((None, tm, d), lambda bi, ti: (bi, ti, 0))
    out_specs = [tok] * 5
    out_shape = [jax.ShapeDtypeStruct((b, t, d), BF16), jax.ShapeDtypeStruct((b, t, d), F32),
                 jax.ShapeDtypeStruct((b, t, d), F32), jax.ShapeDtypeStruct((b, t, d), BF16),
                 jax.ShapeDtypeStruct((b, t, d), BF16)]
    if with_kmean:
        out_specs.append(pl.BlockSpec((None, n_kblk, 1, d), lambda bi, ti: (bi, ti, 0, 0)))
        out_shape.append(jax.ShapeDtypeStruct((b, t // MOBA_BLOCK, 1, d), F32))
    return pl.pallas_call(
        functools.partial(_qkv_kernel, d=d, n_kblk=n_kblk),
        grid=(b, t // tm),
        in_specs=[tok, _mod_spec(mod, layer, 0, tm), _mod_spec(mod, layer, 1, tm),
                  pl.BlockSpec((d, 3 * d), lambda bi, ti: (0, 0))],
        out_specs=out_specs,
        out_shape=out_shape,
        compiler_params=_params("arbitrary", "arbitrary"),
    )(x, mod, mod, w_qkv_b)


def _moba_kernel(slopes_ref, q_ref, k_ref, v_ref, km_ref, o_ref, *, n_blocks, scale):
    hp = pl.program_id(1)
    own = pl.program_id(2)
    tq = MOBA_BLOCK
    q = q_ref[...].astype(F32)
    kmb = km_ref[...].astype(BF16)
    lane = lax.broadcasted_iota(I32, (tq, LANES), 1)
    col = lax.broadcasted_iota(I32, (tq, n_blocks), 1)
    col_f = col.astype(F32)
    rel = (lax.broadcasted_iota(I32, (tq, MOBA_BLOCK), 0)
           - lax.broadcasted_iota(I32, (tq, MOBA_BLOCK), 1)).astype(F32)
    own_start = pl.multiple_of(own * MOBA_BLOCK, MOBA_BLOCK)
    outs = []
    for hh in range(HEADS_PER_LANE_BLOCK):
        slope = slopes_ref[hp * HEADS_PER_LANE_BLOCK + hh]
        in_head = (lane >= hh * HEAD_DIM) & (lane < (hh + 1) * HEAD_DIM)
        qm = jnp.where(in_head, q, 0.0).astype(BF16)
        gate = jnp.where(col < own, _dot_nt(qm, kmb), NEG_INF)
        sel = jnp.zeros((tq, n_blocks), F32)
        for _ in range(MOBA_TOPK):
            m = jnp.max(gate, axis=1, keepdims=True)
            first = jnp.min(jnp.where(gate == m, col_f, float(n_blocks)), axis=1, keepdims=True)
            pick = (col_f == first) & (m > NEG_INF)
            sel = jnp.where(pick, 1.0, sel)
            gate = jnp.where(pick, NEG_INF, gate)

        s = _dot_nt(qm, k_ref[pl.ds(own_start, MOBA_BLOCK), :]) * scale - slope * rel
        s = jnp.where(rel >= 0, s, NEG_INF)
        m_i = jnp.max(s, axis=1, keepdims=True)
        p = jnp.exp(s - m_i)
        l_i = jnp.sum(p, axis=1, keepdims=True)
        acc = _dot(p.astype(BF16), v_ref[pl.ds(own_start, MOBA_BLOCK), :])

        def body(n, carry, qm=qm, sel=sel, slope=slope):
            m_i, l_i, acc = carry
            start = pl.multiple_of(n * MOBA_BLOCK, MOBA_BLOCK)
            chosen = jnp.sum(jnp.where(col == n, sel, 0.0), axis=1, keepdims=True)
            dist = rel + ((own - n) * MOBA_BLOCK).astype(F32)
            s = _dot_nt(qm, k_ref[pl.ds(start, MOBA_BLOCK), :]) * scale - slope * dist
            s = jnp.where(chosen > 0.0, s, NEG_INF)
            m_new = jnp.maximum(m_i, jnp.max(s, axis=1, keepdims=True))
            alpha = jnp.exp(m_i - m_new)
            p = jnp.exp(s - m_new)
            l_new = alpha * l_i + jnp.sum(p, axis=1, keepdims=True)
            acc_new = alpha * acc + _dot(p.astype(BF16), v_ref[pl.ds(start, MOBA_BLOCK), :])
            return m_new, l_new, acc_new

        m_i, l_i, acc = lax.fori_loop(0, own, body, (m_i, l_i, acc))
        outs.append(acc / l_i)
    o_ref[...] = jnp.where(lane < HEAD_DIM, outs[0], outs[1]).astype(BF16)


def _moba_prompt(q, kb, vb, kmean, slopes):
    b, t, d = q.shape
    n_blocks = t // MOBA_BLOCK
    return pl.pallas_call(
        functools.partial(_moba_kernel, n_blocks=n_blocks, scale=HEAD_DIM ** -0.5),
        grid=(b, d // LANES, n_blocks),
        in_specs=[
            pl.BlockSpec(memory_space=pltpu.SMEM),
            pl.BlockSpec((None, MOBA_BLOCK, LANES), lambda bi, hp, qi: (bi, qi, hp)),
            pl.BlockSpec((None, t, LANES), lambda bi, hp, qi: (bi, 0, hp)),
            pl.BlockSpec((None, t, LANES), lambda bi, hp, qi: (bi, 0, hp)),
            pl.BlockSpec((None, n_blocks, LANES), lambda bi, hp, qi: (bi, 0, hp)),
        ],
        out_specs=pl.BlockSpec((None, MOBA_BLOCK, LANES), lambda bi, hp, qi: (bi, qi, hp)),
        out_shape=jax.ShapeDtypeStruct((b, t, d), BF16),
        compiler_params=_params("arbitrary", "arbitrary", "arbitrary"),
    )(slopes, q, kb, vb, kmean)


def _dec_attn_kernel(pt_ref, slopes_ref, q_ref, kn_ref, vn_ref, kp_ref, vp_ref, o_ref,
                     s_scr, p_scr, acc_scr, pown_scr, l_scr, *, n_pages, page, scale):
    del pt_ref
    s_id = pl.program_id(1)
    d = q_ref.shape[-1]
    past = n_pages * page
    n_past_blocks = past // MOBA_BLOCK
    row = lax.broadcasted_iota(I32, (N_HEADS, d), 0)
    head_of_col = lax.broadcasted_iota(I32, (N_HEADS, d), 1) // HEAD_DIM
    q_rows = jnp.where(head_of_col == row, q_ref[...], 0.0)

    @pl.when(s_id < n_pages)
    def _():
        sc = _dot_nt(q_rows.astype(BF16), kp_ref[...].astype(BF16))
        s_scr[:, pl.ds(pl.multiple_of(s_id * page, page), page)] = sc

    @pl.when(s_id == n_pages - 1)
    def _():
        slope = slopes_ref[...]
        blocks = [s_scr[:, n * MOBA_BLOCK:(n + 1) * MOBA_BLOCK] for n in range(n_past_blocks)]
        gates = [jnp.mean(blk, axis=1, keepdims=True) for blk in blocks]
        sel = [jnp.zeros((N_HEADS, 1), F32) for _ in blocks]
        for _ in range(MOBA_TOPK):
            m = functools.reduce(jnp.maximum, gates)
            first = jnp.full((N_HEADS, 1), n_past_blocks, I32)
            for n in reversed(range(n_past_blocks)):
                first = jnp.where(gates[n] == m, n, first)
            for n in range(n_past_blocks):
                pick = (first == n) & (m > NEG_INF)
                sel[n] = jnp.where(pick, 1.0, sel[n])
                gates[n] = jnp.where(pick, NEG_INF, gates[n])
        key_off = lax.broadcasted_iota(I32, (N_HEADS, MOBA_BLOCK), 1)
        logits = []
        for n in range(n_past_blocks):
            dist = (past - n * MOBA_BLOCK - key_off).astype(F32)
            lg = blocks[n] * scale - slope * dist
            logits.append(jnp.where(sel[n] > 0.0, lg, NEG_INF))
        s_own = jnp.sum(q_rows * kn_ref[...], axis=1, keepdims=True) * scale
        m = functools.reduce(jnp.maximum, [jnp.max(lg, axis=1, keepdims=True) for lg in logits] + [s_own])
        l = jnp.exp(s_own - m)
        pown_scr[...] = l
        for n in range(n_past_blocks):
            p = jnp.exp(logits[n] - m)
            l = l + jnp.sum(p, axis=1, keepdims=True)
            p_scr[:, n * MOBA_BLOCK:(n + 1) * MOBA_BLOCK] = p.astype(BF16)
        l_scr[...] = l
        acc_scr[...] = jnp.zeros_like(acc_scr)

    @pl.when(s_id >= n_pages)
    def _():
        pg = s_id - n_pages
        pblk = p_scr[:, pl.ds(pl.multiple_of(pg * page, page), page)]
        acc_scr[...] += _dot(pblk, vp_ref[...].astype(BF16))

    @pl.when(s_id == 2 * n_pages - 1)
    def _():
        tot = (acc_scr[...] + pown_scr[...] * vn_ref[...]) / l_scr[...]
        o_ref[...] = jnp.sum(jnp.where(head_of_col == row, tot, 0.0), axis=0, keepdims=True)


def _moba_decode(q, k_new, v_new, cache_k, cache_v, att_layer, page_table, slopes):
    db, _, d = q.shape
    n_pages = page_table.shape[1]
    page = cache_k.shape[2]
    past = n_pages * page
    assert past % MOBA_BLOCK == 0 and page % LANES == 0
    row = pl.BlockSpec((None, 1, d), lambda b, s, pt: (b, 0, 0))

    def k_map(b, s, pt):
        return (att_layer, pt[b * n_pages + jnp.minimum(s, n_pages - 1)], 0, 0)

    def v_map(b, s, pt):
        return (att_layer, pt[b * n_pages + jnp.maximum(s - n_pages, 0)], 0, 0)

    return pl.pallas_call(
        functools.partial(_dec_attn_kernel, n_pages=n_pages, page=page, scale=HEAD_DIM ** -0.5),
        grid_spec=pltpu.PrefetchScalarGridSpec(
            num_scalar_prefetch=1,
            grid=(db, 2 * n_pages),
            in_specs=[
                pl.BlockSpec((N_HEADS, 1), lambda b, s, pt: (0, 0)),
                row, row, row,
                pl.BlockSpec((None, None, page, d), k_map),
                pl.BlockSpec((None, None, page, d), v_map),
            ],
            out_specs=row,
            scratch_shapes=[
                pltpu.VMEM((N_HEADS, past), F32),
                pltpu.VMEM((N_HEADS, past), BF16),
                pltpu.VMEM((N_HEADS, d), F32),
                pltpu.VMEM((N_HEADS, 1), F32),
                pltpu.VMEM((N_HEADS, 1), F32),
            ],
        ),
        out_shape=jax.ShapeDtypeStruct((db, 1, d), F32),
        compiler_params=_params("arbitrary", "arbitrary"),
    )(page_table.reshape(-1), slopes.reshape(N_HEADS, 1), q, k_new, v_new, cache_k, cache_v)


def _pool_counts(pos, width):
    return jnp.minimum(pos + 1, width).astype(F32)


def _pool_prompt_kernel(x_ref, sh_ref, sc_ref, o_ref, hist_ref, ext, buf_a, buf_b, *, tm, group):
    t = pl.program_id(1)
    rows = POOL_HALO + tm

    @pl.when(t == 0)
    def _():
        ext[0:POOL_HALO, :] = jnp.zeros((POOL_HALO, ext.shape[1]), F32)

    h = x_ref[...] * (1.0 + sc_ref[...]) + sh_ref[...]
    ext[POOL_HALO:rows, :] = h
    d = h.shape[1]
    buf_a[8:rows, :] = ext[8:rows, :] + ext[7:rows - 1, :]
    buf_b[16:rows, group:d] = buf_a[16:rows, group:d] + buf_a[14:rows - 2, group:d]
    buf_a[24:rows, 2 * group:d] = buf_b[24:rows, 2 * group:d] + buf_b[20:rows - 4, 2 * group:d]
    buf_b[32:rows, 3 * group:d] = buf_a[32:rows, 3 * group:d] + buf_a[24:rows - 8, 3 * group:d]
    pos = t * tm + lax.broadcasted_iota(I32, (tm, 1), 0)
    sums = (buf_a, buf_b, buf_a, buf_b)
    for g, width in enumerate(POOL_WINDOWS):
        cols = slice(g * group, (g + 1) * group)
        mean = sums[g][POOL_HALO:rows, cols] / _pool_counts(pos, width)
        o_ref[:, cols] = (mean - h[:, cols]).astype(BF16)
    ext[0:POOL_HALO, :] = h[tm - POOL_HALO:tm, :]

    @pl.when(t == pl.num_programs(1) - 1)
    def _():
        hist_ref[...] = h[tm - POOL_HIST:tm, :]


def _pool_prompt(x, mod, layer, tm):
    b, t, d = x.shape
    tok = pl.BlockSpec((None, tm, d), lambda bi, ti: (bi, ti, 0))
    return pl.pallas_call(
        functools.partial(_pool_prompt_kernel, tm=tm, group=d // len(POOL_WINDOWS)),
        grid=(b, t // tm),
        in_specs=[tok, _mod_spec(mod, layer, 0, tm), _mod_spec(mod, layer, 1, tm)],
        out_specs=[tok, pl.BlockSpec((None, POOL_HIST, d), lambda bi, ti: (bi, 0, 0))],
        out_shape=[jax.ShapeDtypeStruct((b, t, d), BF16), jax.ShapeDtypeStruct((b, POOL_HIST, d), F32)],
        scratch_shapes=[pltpu.VMEM((POOL_HALO + tm, d), F32)] * 3,
        compiler_params=_params("arbitrary", "arbitrary"),
    )(x, mod, mod)


def _pool_decode_kernel(x_ref, sh_ref, sc_ref, hist_ref, o_ref, hist_out_ref, *, group, pos0):
    h = x_ref[...] * (1.0 + sc_ref[...]) + sh_ref[...]
    rows = []
    for b in range(h.shape[0]):
        hb = h[b:b + 1, :]
        parts = []
        for g, width in enumerate(POOL_WINDOWS):
            cols = slice(g * group, (g + 1) * group)
            tail = hist_ref[b, POOL_HIST - (width - 1):POOL_HIST, cols]
            total = jnp.sum(tail, axis=0, keepdims=True) + hb[:, cols]
            parts.append(total / float(min(pos0 + 1, width)) - hb[:, cols])
        rows.append(jnp.concatenate(parts, axis=1))
        hist_out_ref[b, 0:POOL_HIST - 1, :] = hist_ref[b, 1:POOL_HIST, :]
        hist_out_ref[b, POOL_HIST - 1:POOL_HIST, :] = hb
    o_ref[...] = jnp.concatenate(rows, axis=0).astype(BF16)


def _pool_decode(x, mod, layer, hist, pos0):
    _, db, d = x.shape
    bb = 16
    assert db % bb == 0 and pos0 >= POOL_HIST
    tok = pl.BlockSpec((None, bb, d), lambda bi, ti: (0, ti, 0))
    hspec = pl.BlockSpec((bb, POOL_HIST, d), lambda bi, ti: (ti, 0, 0))
    mixed, hist_new = pl.pallas_call(
        functools.partial(_pool_decode_kernel, group=d // len(POOL_WINDOWS), pos0=pos0),
        grid=(1, db // bb),
        in_specs=[tok, _mod_spec(mod, layer, 0, bb), _mod_spec(mod, layer, 1, bb), hspec],
        out_specs=[tok, hspec],
        out_shape=[jax.ShapeDtypeStruct((1, db, d), BF16), jax.ShapeDtypeStruct((db, POOL_HIST, d), F32)],
        compiler_params=_params("arbitrary", "arbitrary"),
    )(x, mod, mod, hist)
    return mixed, hist_new


def _layer_norm(z, g, b):
    mu = jnp.mean(z, axis=-1, keepdims=True)
    zc = z - mu
    var = jnp.mean(zc * zc, axis=-1, keepdims=True)
    return zc * lax.rsqrt(var + LN_EPS) * g + b


def _first_max(vals):
    m = functools.reduce(jnp.maximum, vals)
    idx = jnp.full(m.shape, len(vals) - 1, I32)
    for i in reversed(range(len(vals) - 1)):
        idx = jnp.where(vals[i] == m, i, idx)
    return m, idx


def _top2(vals):
    m0, i0 = _first_max(vals)
    rest = [jnp.where(i0 == i, NEG_INF, v) for i, v in enumerate(vals)]
    m1, i1 = _first_max(rest)
    return m0, i0, m1, i1


def _post_kernel(o_ref, x_ref, g1_ref, sh2_ref, sc2_ref, w_ref, cs_ref, lng_ref, lnb_ref,
                 wr_hi_ref, wr_lo_ref, br_ref, xo_ref, h2_ref, rt_ref, *, alpha):
    out = _dot(o_ref[...], w_ref[...]) * cs_ref[...]
    xn = _layer_norm(alpha * x_ref[...] + g1_ref[...] * out, lng_ref[...], lnb_ref[...])
    xo_ref[...] = xn
    h2 = xn * (1.0 + sc2_ref[...]) + sh2_ref[...]
    h2_ref[...] = h2
    hi, lo = _split_bf16(h2)
    logits = _dot_nt(wr_hi_ref[...], hi) + _dot_nt(wr_hi_ref[...], lo) + _dot_nt(wr_lo_ref[...], hi) + br_ref[...]
    aff = jax.nn.sigmoid(logits)
    rows = [aff[e:e + 1, :] for e in range(N_EXPERTS)]
    gscores = []
    for g in range(N_EXPERT_GROUPS):
        m0, _, m1, _ = _top2(rows[g * EXPERTS_PER_GROUP:(g + 1) * EXPERTS_PER_GROUP])
        gscores.append(m0 + m1)
    _, g_sel = _first_max(gscores)
    in_grp = []
    for i in range(EXPERTS_PER_GROUP):
        v = rows[i]
        for g in range(1, N_EXPERT_GROUPS):
            v = jnp.where(g_sel == g, rows[g * EXPERTS_PER_GROUP + i], v)
        in_grp.append(v)
    w0, i0, w1, i1 = _top2(in_grp)
    denom = w0 + w1
    e0 = (g_sel * EXPERTS_PER_GROUP + i0).astype(F32)
    e1 = (g_sel * EXPERTS_PER_GROUP + i1).astype(F32)
    zero = jnp.zeros_like(w0)
    rt_ref[...] = jnp.concatenate([e0, e1, w0 / denom, w1 / denom, zero, zero, zero, zero], axis=0)


def _post_mixer(o, x, mod, layer, w_b, colscale, ln_g, ln_b, wr_hi, wr_lo, b_router, alpha, tm):
    b, t, d = x.shape
    tok = pl.BlockSpec((None, tm, d), lambda bi, ti: (bi, ti, 0))
    vec = pl.BlockSpec((1, d), lambda bi, ti: (0, 0))
    rtr = pl.BlockSpec((N_EXPERTS, d), lambda bi, ti: (0, 0))
    return pl.pallas_call(
        functools.partial(_post_kernel, alpha=alpha),
        grid=(b, t // tm),
        in_specs=[tok, tok, _mod_spec(mod, layer, 2, tm), _mod_spec(mod, layer, 3, tm), _mod_spec(mod, layer, 4, tm),
                  pl.BlockSpec((d, d), lambda bi, ti: (0, 0)), vec, vec, vec, rtr, rtr,
                  pl.BlockSpec((N_EXPERTS, 1), lambda bi, ti: (0, 0))],
        out_specs=[tok, tok, pl.BlockSpec((None, 8, tm), lambda bi, ti: (bi, 0, ti))],
        out_shape=[jax.ShapeDtypeStruct((b, t, d), F32), jax.ShapeDtypeStruct((b, t, d), F32),
                   jax.ShapeDtypeStruct((b, 8, t), F32)],
        compiler_params=_params("arbitrary", "arbitrary"),
    )(o, x, mod, mod, mod, w_b, colscale, ln_g, ln_b, wr_hi, wr_lo, b_router)


def _moe_kernel(be_ref, nact_ref, src_ref, dst_ref, gate_ref, h_hbm, wg_ref, wu_ref, wd_ref, y_hbm,
                xg, yb, wgb, wub, wdb, sem_in, sem_out):
    i = pl.program_id(0)
    tb = xg.shape[0]

    def gather(r):
        return pltpu.make_async_copy(h_hbm.at[pl.ds(src_ref[0, 0, r], 1), :], xg.at[pl.ds(r, 1), :], sem_in)

    def scatter(r):
        return pltpu.make_async_copy(yb.at[pl.ds(r, 1), :], y_hbm.at[pl.ds(dst_ref[0, 0, r], 1), :], sem_out)

    def each_row(fn):
        def body(r, carry):
            fn(r)
            return carry
        lax.fori_loop(0, tb, body, 0)

    @pl.when(i < nact_ref[0])
    def _():
        each_row(lambda r: gather(r).start())
        changed = jnp.logical_or(i == 0, be_ref[i] != be_ref[jnp.maximum(i - 1, 0)])

        @pl.when(changed)
        def _():
            wgb[...] = wg_ref[...].astype(BF16)
            wub[...] = wu_ref[...].astype(BF16)
            wdb[...] = wd_ref[...].astype(BF16)

        each_row(lambda r: gather(r).wait())
        x = xg[...].astype(BF16)
        a = _dot(x, wgb[...])
        a = a * jax.nn.sigmoid(a) * _dot(x, wub[...])
        yb[...] = _dot(a.astype(BF16), wdb[...]) * gate_ref[...]
        each_row(lambda r: pl.when(dst_ref[0, 0, r] >= 0)(lambda: scatter(r).start()))
        each_row(lambda r: pl.when(dst_ref[0, 0, r] >= 0)(lambda: scatter(r).wait()))


def _moe_experts(h2_flat, blk_expert, n_active, src_tok, dst_row, gate_rows, w_gate, w_up, w_down, layer, n_out_rows):
    n_rows = src_tok.shape[0]
    tb = MOE_ROWS
    n_blk = n_rows // tb
    d = h2_flat.shape[1]
    de = w_gate.shape[3]
    idx = pl.BlockSpec((1, 1, tb), lambda i, be, na: (i, 0, 0), memory_space=pltpu.SMEM)
    return pl.pallas_call(
        _moe_kernel,
        grid_spec=pltpu.PrefetchScalarGridSpec(
            num_scalar_prefetch=2,
            grid=(n_blk,),
            in_specs=[
                idx, idx,
                pl.BlockSpec((tb, 1), lambda i, be, na: (i, 0)),
                pl.BlockSpec(memory_space=pl.ANY),
                pl.BlockSpec((None, None, d, de), lambda i, be, na: (layer, be[i], 0, 0)),
                pl.BlockSpec((None, None, d, de), lambda i, be, na: (layer, be[i], 0, 0)),
                pl.BlockSpec((None, None, de, d), lambda i, be, na: (layer, be[i], 0, 0)),
            ],
            out_specs=pl.BlockSpec(memory_space=pl.ANY),
            scratch_shapes=[
                pltpu.VMEM((tb, d), F32), pltpu.VMEM((tb, d), F32),
                pltpu.VMEM((d, de), BF16), pltpu.VMEM((d, de), BF16), pltpu.VMEM((de, d), BF16),
                pltpu.SemaphoreType.DMA(()), pltpu.SemaphoreType.DMA(()),
            ],
        ),
        out_shape=jax.ShapeDtypeStruct((n_out_rows, d), F32),
        compiler_params=_params("arbitrary"),
    )(blk_expert, n_active, src_tok.reshape(n_blk, 1, tb), dst_row.reshape(n_blk, 1, tb),
      gate_rows.reshape(n_rows, 1), h2_flat, w_gate, w_up, w_down)


def _route_tables(rt, n_tok):
    tb = MOE_ROWS
    m = 2 * n_tok
    e = jnp.stack([rt[:, 0, :], rt[:, 1, :]], axis=-1).reshape(m).astype(I32)
    g = jnp.stack([rt[:, 2, :], rt[:, 3, :]], axis=-1).reshape(m)
    onehot = (e[:, None] == jnp.arange(N_EXPERTS, dtype=I32)[None, :]).astype(I32)
    counts = onehot.sum(0)
    rank = jnp.sum((jnp.cumsum(onehot, axis=0) - onehot) * onehot, axis=1)
    padded = (counts + tb - 1) // tb * tb
    pad_ends = jnp.cumsum(padded)
    pad_starts = pad_ends - padded
    dest = pad_starts[e] + rank
    n_blk = -(-(m + N_EXPERTS * (tb - 1)) // tb)
    n_rows = n_blk * tb
    pair = jnp.arange(m, dtype=I32)
    src_tok = jnp.zeros((n_rows,), I32).at[dest].set(pair // 2)
    dst_row = jnp.full((n_rows,), -1, I32).at[dest].set(pair)
    gate_rows = jnp.zeros((n_rows,), F32).at[dest].set(g)
    blk_expert = jnp.minimum(jnp.searchsorted(pad_ends, jnp.arange(n_blk, dtype=I32) * tb, side="right"),
                             N_EXPERTS - 1).astype(I32)
    n_active = (pad_ends[-1] // tb).astype(I32).reshape(1)
    return blk_expert, n_active, src_tok, dst_row, gate_rows, m


def _final_kernel(x_ref, y_ref, g2_ref, lng_ref, lnb_ref, o_ref, *, alpha, d):
    ffn = y_ref[:, 0:d] + y_ref[:, d:2 * d]
    o_ref[...] = _layer_norm(alpha * x_ref[...] + g2_ref[...] * ffn, lng_ref[...], lnb_ref[...])


def _final_ln(x, y_pairs, mod, layer, ln_g, ln_b, alpha, tm):
    b, t, d = x.shape
    tok = pl.BlockSpec((None, tm, d), lambda bi, ti: (bi, ti, 0))
    vec = pl.BlockSpec((1, d), lambda bi, ti: (0, 0))
    tiles = t // tm
    return pl.pallas_call(
        functools.partial(_final_kernel, alpha=alpha, d=d),
        grid=(b, tiles),
        in_specs=[tok, pl.BlockSpec((tm, 2 * d), lambda bi, ti: (bi * tiles + ti, 0)),
                  _mod_spec(mod, layer, 5, tm), vec, vec],
        out_specs=tok,
        out_shape=jax.ShapeDtypeStruct((b, t, d), F32),
        compiler_params=_params("arbitrary", "arbitrary"),
    )(x, y_pairs, mod, ln_g, ln_b)


def _run_trunk(x, mod, pos0, cache_k, cache_v, page_table, pool_state, prm, tm):
    b, t, d = x.shape
    depth = prm["depth"]
    alpha = (2.0 * depth) ** 0.25
    decode = cache_k is not None
    new_k, new_v, new_pool = [], [], []
    for i in range(depth):
        j = i // 2
        if i % 2 == 0:
            q, k, v, kb, vb, *km = _qkv(x, mod, i, prm["w_qkv_b"][j], tm, with_kmean=not decode)
            new_k.append(k)
            new_v.append(v)
            if decode:
                o = _moba_decode(q.astype(F32).reshape(t, 1, d), k.reshape(t, 1, d), v.reshape(t, 1, d),
                                 cache_k, cache_v, j, page_table, prm["slopes"])
                o = o.reshape(1, t, d).astype(BF16)
            else:
                o = _moba_prompt(q, kb, vb, km[0].reshape(b, t // MOBA_BLOCK, d), prm["slopes"])
            w_b, colscale = prm["w_o_b"][j], prm["ones"]
        else:
            if decode:
                o, hist_new = _pool_decode(x, mod, i, pool_state[j], pos0)
            else:
                o, hist_new = _pool_prompt(x, mod, i, tm)
            new_pool.append(hist_new)
            w_b, colscale = prm["w_pool_b"][j], prm["pool_scale"][j]
        x, h2, rt = _post_mixer(o, x, mod, i, w_b, colscale, prm["ln_g"][i, 0], prm["ln_b"][i, 0],
                                prm["wr_hi"], prm["wr_lo"], prm["b_router"], alpha, tm)
        n_tok = b * t
        blk_expert, n_active, src_tok, dst_row, gate_rows, n_out_rows = _route_tables(rt, n_tok)
        y = _moe_experts(h2.reshape(n_tok, d), blk_expert, n_active, src_tok, dst_row, gate_rows,
                         prm["w_gate"], prm["w_up"], prm["w_down"], i, n_out_rows)
        x = _final_ln(x, y.reshape(n_out_rows // 2, 2 * d), mod, i, prm["ln_g"][i, 1], prm["ln_b"][i, 1], alpha, tm)
    return x, new_k, new_v, new_pool


def kernel(x_prompt, x_sample, cache_k, cache_v, state_pool, page_table, c_prompt, c_sample, w_ada, b_ada, ln_g, ln_b, w_qkv, w_o, w_pool, pool_scale, w_router, b_router, w_gate, w_up, w_down):
    bp, seq, d = x_prompt.shape
    db, dec_seq, _ = x_sample.shape
    depth = w_ada.shape[0]
    assert dec_seq == 1 and d == N_HEADS * HEAD_DIM and seq % MOBA_BLOCK == 0
    n_att, n_phys, page = cache_k.shape[0], cache_k.shape[1], cache_k.shape[2]
    pos0 = page_table.shape[1] * page

    c_all = jnp.concatenate([c_prompt, c_sample], axis=0)
    pad = (-c_all.shape[0]) % 8
    c_all = jnp.pad(c_all, ((0, pad), (0, 0)))
    mod = _adaln(c_all, w_ada, b_ada).reshape(depth, c_all.shape[0], 6, d).transpose(0, 2, 1, 3)
    mod_p = mod[:, :, :bp].reshape(depth, 6, bp, 1, d)
    mod_s = mod[:, :, bp:bp + db].reshape(depth, 6, 1, db, d)

    group = d // len(POOL_WINDOWS)
    w_pool_dense = jnp.zeros((w_pool.shape[0], d, d), F32)
    for g in range(len(POOL_WINDOWS)):
        w_pool_dense = w_pool_dense.at[:, g * group:(g + 1) * group, g * group:(g + 1) * group].set(w_pool[:, g])
    wr_hi, wr_lo = _split_bf16(w_router.T)
    prm = dict(
        depth=depth,
        w_qkv_b=w_qkv.astype(BF16), w_o_b=w_o.astype(BF16), w_pool_b=w_pool_dense.astype(BF16),
        pool_scale=pool_scale.reshape(-1, 1, d), ones=jnp.ones((1, d), F32),
        ln_g=ln_g.reshape(depth, 2, 1, d), ln_b=ln_b.reshape(depth, 2, 1, d),
        wr_hi=wr_hi, wr_lo=wr_lo, b_router=b_router.reshape(N_EXPERTS, 1),
        w_gate=w_gate, w_up=w_up, w_down=w_down,
        slopes=2.0 ** (-(8.0 / N_HEADS) * jnp.arange(1, N_HEADS + 1, dtype=F32)),
    )

    y_p, k_p, v_p, pool_p = _run_trunk(x_prompt, mod_p, 0, None, None, None, None, prm, tm=MOBA_BLOCK)
    y_s, k_s, v_s, pool_s = _run_trunk(
        x_sample.reshape(1, db, d), mod_s, pos0,
        cache_k.reshape(n_att, n_phys, page, d), cache_v.reshape(n_att, n_phys, page, d),
        page_table, state_pool, prm, tm=db)

    def heads(ts, lead):
        return jnp.stack(ts).reshape(len(ts), *lead, N_HEADS, HEAD_DIM)

    return (y_p, y_s.reshape(db, 1, d),
            heads(k_p, (bp, seq)), heads(v_p, (bp, seq)), jnp.stack(pool_p),
            heads(k_s, (db, 1)), heads(v_s, (db, 1)), jnp.stack(pool_s))
```

```python
import functools

import jax
import jax.numpy as jnp
from jax import lax
from jax.experimental import pallas as pl
from jax.experimental.pallas import tpu as pltpu

F32 = jnp.float32
BF16 = jnp.bfloat16
I32 = jnp.int32

N_HEADS = 16
HEAD_DIM = 64
MOBA_BLOCK = 256
MOBA_TOPK = 3
POOL_WINDOWS = (2, 4, 8, 16)
POOL_HIST = max(POOL_WINDOWS) - 1
POOL_HALO = 32
N_EXPERTS = 16
N_EXPERT_GROUPS = 4
EXPERTS_PER_GROUP = N_EXPERTS // N_EXPERT_GROUPS
LN_EPS = 1e-5
LANES = 128
ATT_HEADS = 4
ATT_LANES = ATT_HEADS * HEAD_DIM
MOE_ROWS = 256
VMEM_LIMIT = 48 * 1024 * 1024
NEG_INF = float("-inf")
LOG2E = 1.4426950408889634
Q_PRESCALE = HEAD_DIM ** -0.5 * LOG2E


def _dot(a, b):
    return jnp.dot(a, b, preferred_element_type=F32)


def _dot_nt(a, b):
    return lax.dot_general(a, b, (((1,), (1,)), ((), ())), preferred_element_type=F32)


def _split_bf16(x):
    hi = x.astype(BF16)
    lo = (x - hi.astype(F32)).astype(BF16)
    return hi, lo


def _params(*sem):
    return pltpu.CompilerParams(dimension_semantics=sem, vmem_limit_bytes=VMEM_LIMIT)


def _adaln_kernel(c_ref, w_ref, b_ref, o_ref):
    c = c_ref[...]
    s_hi, s_lo = _split_bf16(c * jax.nn.sigmoid(c))
    w_hi, w_lo = _split_bf16(w_ref[...])
    o_ref[...] = _dot(s_hi, w_hi) + _dot(s_lo, w_hi) + _dot(s_hi, w_lo) + b_ref[...]


def _adaln(c, w_ada, b_ada):
    depth, d, d6 = w_ada.shape
    bc = c.shape[0]
    tn = 1024
    return pl.pallas_call(
        _adaln_kernel,
        grid=(depth, d6 // tn),
        in_specs=[
            pl.BlockSpec((bc, d), lambda l, j: (0, 0)),
            pl.BlockSpec((None, d, tn), lambda l, j: (l, 0, j)),
            pl.BlockSpec((None, 1, tn), lambda l, j: (l, 0, j)),
        ],
        out_specs=pl.BlockSpec((None, bc, tn), lambda l, j: (l, 0, j)),
        out_shape=jax.ShapeDtypeStruct((depth, bc, d6), F32),
        compiler_params=_params("arbitrary", "arbitrary"),
    )(c, w_ada, b_ada.reshape(depth, 1, d6))


def _mod_spec(mod, layer, which, tm):
    r = mod.shape[3]
    d = mod.shape[4]
    if r == 1:
        return pl.BlockSpec((None, None, None, 1, d), lambda b, t: (layer, which, b, 0, 0))
    return pl.BlockSpec((None, None, None, tm, d), lambda b, t: (layer, which, b, t, 0))


def _qkv_kernel(x_ref, sh_ref, sc_ref, w_ref, k_ref, v_ref, *extra_refs, d, n_kblk):
    h = x_ref[...] * (1.0 + sc_ref[...]) + sh_ref[...]
    hb = h.astype(BF16)
    q = _dot(hb, w_ref[:, 0:d]) * Q_PRESCALE
    k = _dot(hb, w_ref[:, d:2 * d])
    v = _dot(hb, w_ref[:, 2 * d:3 * d])
    k_ref[...] = k
    v_ref[...] = v
    if n_kblk == 0:
        extra_refs[0][...] = q.astype(BF16)
        return
    qt_ref, vt_ref, kh_ref, km_ref = extra_refs
    qt_ref[...] = q.T.astype(BF16)
    vt_ref[...] = v.T.astype(BF16)
    for hd in range(N_HEADS):
        cols = slice(hd * HEAD_DIM, (hd + 1) * HEAD_DIM)
        kh_ref[hd] = k[:, cols].astype(BF16)
        for i in range(n_kblk):
            km_ref[i, hd:hd + 1, :] = jnp.mean(k[i * MOBA_BLOCK:(i + 1) * MOBA_BLOCK, cols], axis=0, keepdims=True)


def _qkv(x, mod, layer, w_qkv_b, tm, prompt):
    b, t, d = x.shape
    n_kblk = tm // MOBA_BLOCK if prompt else 0
    tok = pl.BlockSpec((None, tm, d), lambda bi, ti: (bi, ti, 0))
    out_specs = [tok] * 2
    out_shape = [jax.ShapeDtypeStruct((b, t, d), F32)] * 2
    if prompt:
        tr = pl.BlockSpec((None, d, tm), lambda bi, ti: (bi, 0, ti))
        out_specs += [tr, tr, pl.BlockSpec((None, N_HEADS, tm, HEAD_DIM), lambda bi, ti: (bi, 0, ti, 0)),
                      pl.BlockSpec((None, n_kblk, N_HEADS, HEAD_DIM), lambda bi, ti: (bi, ti, 0, 0))]
        out_shape += [jax.ShapeDtypeStruct((b, d, t), BF16)] * 2
        out_shape += [jax.ShapeDtypeStruct((b, N_HEADS, t, HEAD_DIM), BF16),
                      jax.ShapeDtypeStruct((b, t // MOBA_BLOCK, N_HEADS, HEAD_DIM), F32)]
    else:
        out_specs.append(tok)
        out_shape.append(jax.ShapeDtypeStruct((b, t, d), BF16))
    return pl.pallas_call(
        functools.partial(_qkv_kernel, d=d, n_kblk=n_kblk),
        grid=(b, t // tm),
        in_specs=[tok, _mod_spec(mod, layer, 0, tm), _mod_spec(mod, layer, 1, tm),
                  pl.BlockSpec((d, 3 * d), lambda bi, ti: (0, 0))],
        out_specs=out_specs,
        out_shape=out_shape,
        compiler_params=_params("arbitrary", "arbitrary"),
    )(x, mod, mod, w_qkv_b)


def _moba_kernel(offs_ref, qt_ref, k_ref, vt_ref, km_ref, bias_ref, o_ref, sel_scr, *, n_blocks):
    hp = pl.program_id(1)
    own = pl.program_id(2)
    tq = MOBA_BLOCK
    blk = lax.broadcasted_iota(I32, (n_blocks, tq), 0)
    blk_f = blk.astype(F32)
    causal = lax.broadcasted_iota(I32, (MOBA_BLOCK, tq), 0) <= lax.broadcasted_iota(I32, (MOBA_BLOCK, tq), 1)
    own_start = pl.multiple_of(own * MOBA_BLOCK, MOBA_BLOCK)
    heads = range(ATT_HEADS)
    rows = [slice(hh * HEAD_DIM, (hh + 1) * HEAD_DIM) for hh in heads]
    block_step = [offs_ref[hp * ATT_HEADS + hh] for hh in heads]
    qt = [qt_ref[r, :] for r in rows]

    state = []
    for hh in heads:
        gate = jnp.where(blk < own, _dot(km_ref[hh].astype(BF16), qt[hh]), NEG_INF)
        sel = jnp.zeros((n_blocks, tq), F32)
        for _ in range(MOBA_TOPK):
            m = jnp.max(gate, axis=0, keepdims=True)
            first = jnp.min(jnp.where(gate == m, blk_f, float(n_blocks)), axis=0, keepdims=True)
            pick = (blk_f == first) & (m > NEG_INF)
            sel = jnp.where(pick, 1.0, sel)
            gate = jnp.where(pick, NEG_INF, gate)
        sel_scr[hh] = sel

        t = _dot(k_ref[hh, pl.ds(own_start, MOBA_BLOCK), :], qt[hh]) + bias_ref[hh]
        t = jnp.where(causal, t, NEG_INF)
        m_i = jnp.max(t, axis=0, keepdims=True)
        p = jnp.exp2(t - m_i)
        l_i = jnp.sum(p, axis=0, keepdims=True)
        acc = _dot(vt_ref[rows[hh], pl.ds(own_start, MOBA_BLOCK)], p.astype(BF16))
        state.append((m_i, l_i, acc))

    def body(n, state):
        start = pl.multiple_of(n * MOBA_BLOCK, MOBA_BLOCK)
        blocks_back = (own - n).astype(F32)
        new_state = []
        scores = [_dot(k_ref[hh, pl.ds(start, MOBA_BLOCK), :], qt[hh]) for hh in heads]
        for hh in heads:
            m_i, l_i, acc = state[hh]
            off = blocks_back * block_step[hh]
            t = scores[hh] + bias_ref[hh]
            chosen = sel_scr[hh, pl.ds(n, 1), :] > 0.0
            m_new = jnp.maximum(m_i, jnp.where(chosen, jnp.max(t, axis=0, keepdims=True) - off, NEG_INF))
            alpha = jnp.exp2(m_i - m_new)
            p = jnp.exp2(t - jnp.where(chosen, m_new + off, float("inf")))
            l_new = alpha * l_i + jnp.sum(p, axis=0, keepdims=True)
            acc_new = alpha * acc + _dot(vt_ref[rows[hh], pl.ds(start, MOBA_BLOCK)], p.astype(BF16))
            new_state.append((m_new, l_new, acc_new))
        return tuple(new_state)

    state = lax.fori_loop(0, own, body, tuple(state))
    out = jnp.concatenate([acc / l_i for _, l_i, acc in state], axis=0)
    o_ref[...] = out.T.astype(BF16)


def _moba_prompt(qt, kh, vt, kmean, bias, block_steps):
    b, d, t = qt.shape
    n_blocks = t // MOBA_BLOCK
    return pl.pallas_call(
        functools.partial(_moba_kernel, n_blocks=n_blocks),
        grid=(b, d // ATT_LANES, n_blocks),
        in_specs=[
            pl.BlockSpec(memory_space=pltpu.SMEM),
            pl.BlockSpec((None, ATT_LANES, MOBA_BLOCK), lambda bi, hp, qi: (bi, hp, qi)),
            pl.BlockSpec((None, ATT_HEADS, t, HEAD_DIM), lambda bi, hp, qi: (bi, hp, 0, 0)),
            pl.BlockSpec((None, ATT_LANES, t), lambda bi, hp, qi: (bi, hp, 0)),
            pl.BlockSpec((None, ATT_HEADS, n_blocks, HEAD_DIM), lambda bi, hp, qi: (bi, hp, 0, 0)),
            pl.BlockSpec((ATT_HEADS, MOBA_BLOCK, MOBA_BLOCK), lambda bi, hp, qi: (hp, 0, 0)),
        ],
        out_specs=pl.BlockSpec((None, MOBA_BLOCK, ATT_LANES), lambda bi, hp, qi: (bi, qi, hp)),
        out_shape=jax.ShapeDtypeStruct((b, t, d), BF16),
        scratch_shapes=[pltpu.VMEM((ATT_HEADS, n_blocks, MOBA_BLOCK), F32)],
        compiler_params=_params("arbitrary", "arbitrary", "arbitrary"),
    )(block_steps, qt, kh, vt, kmean, bias)


def _dec_attn_kernel(pt_ref, q_ref, kn_ref, vn_ref, bias_ref, kp_ref, vp_ref, o_ref,
                     s_scr, p_scr, acc_scr, pown_scr, l_scr, *, n_pages, page):
    del pt_ref
    s_id = pl.program_id(1)
    lanes_per_page = page * N_HEADS
    lanes_per_block = MOBA_BLOCK * N_HEADS
    n_past_blocks = n_pages * page // MOBA_BLOCK
    q = q_ref[...]

    def lanes(n):
        return slice(n * lanes_per_block, (n + 1) * lanes_per_block)

    @pl.when(s_id < n_pages)
    def _():
        rows = kp_ref[...].reshape(lanes_per_page, HEAD_DIM).astype(BF16)
        start = pl.multiple_of(s_id * lanes_per_page, lanes_per_page)
        s_scr[:, pl.ds(start, lanes_per_page)] = _dot_nt(q.astype(BF16), rows)

    @pl.when(s_id == n_pages - 1)
    def _():
        gates = []
        for n in range(n_past_blocks):
            mine = bias_ref[:, lanes(n)] > NEG_INF
            gates.append(jnp.sum(jnp.where(mine, s_scr[:, lanes(n)], 0.0), axis=1, keepdims=True) / MOBA_BLOCK)
        sel = [jnp.zeros((N_HEADS, 1), F32) for _ in gates]
        for _ in range(MOBA_TOPK):
            m = functools.reduce(jnp.maximum, gates)
            first = jnp.full((N_HEADS, 1), n_past_blocks, I32)
            for n in reversed(range(n_past_blocks)):
                first = jnp.where(gates[n] == m, n, first)
            for n in range(n_past_blocks):
                pick = (first == n) & (m > NEG_INF)
                sel[n] = jnp.where(pick, 1.0, sel[n])
                gates[n] = jnp.where(pick, NEG_INF, gates[n])

        def logits(n):
            return jnp.where(sel[n] > 0.0, s_scr[:, lanes(n)] + bias_ref[:, lanes(n)], NEG_INF)

        s_own = jnp.sum(q * kn_ref[...], axis=1, keepdims=True)
        m = functools.reduce(jnp.maximum, [jnp.max(logits(n), axis=1, keepdims=True)
                                           for n in range(n_past_blocks)] + [s_own])
        l = jnp.exp2(s_own - m)
        pown_scr[...] = l
        for n in range(n_past_blocks):
            p = jnp.exp2(logits(n) - m)
            l = l + jnp.sum(p, axis=1, keepdims=True)
            p_scr[:, lanes(n)] = p.astype(BF16)
        l_scr[...] = l
        acc_scr[...] = jnp.zeros_like(acc_scr)

    @pl.when(s_id >= n_pages)
    def _():
        start = pl.multiple_of((s_id - n_pages) * lanes_per_page, lanes_per_page)
        rows = vp_ref[...].reshape(lanes_per_page, HEAD_DIM).astype(BF16)
        acc_scr[...] += _dot(p_scr[:, pl.ds(start, lanes_per_page)], rows)

    @pl.when(s_id == 2 * n_pages - 1)
    def _():
        o_ref[...] = (acc_scr[...] + pown_scr[...] * vn_ref[...]) / l_scr[...]


def _moba_decode(q, k_new, v_new, cache_k, cache_v, att_layer, page_table, bias):
    db = q.shape[0]
    n_pages = page_table.shape[1]
    page = cache_k.shape[2]
    past = n_pages * page
    assert past % MOBA_BLOCK == 0 and MOBA_BLOCK % page == 0
    row = pl.BlockSpec((None, N_HEADS, HEAD_DIM), lambda b, s, pt: (b, 0, 0))
    page_block = (None, None, page, N_HEADS, HEAD_DIM)

    def k_map(b, s, pt):
        return (att_layer, pt[b * n_pages + jnp.minimum(s, n_pages - 1)], 0, 0, 0)

    def v_map(b, s, pt):
        return (att_layer, pt[b * n_pages + jnp.maximum(s - n_pages, 0)], 0, 0, 0)

    return pl.pallas_call(
        functools.partial(_dec_attn_kernel, n_pages=n_pages, page=page),
        grid_spec=pltpu.PrefetchScalarGridSpec(
            num_scalar_prefetch=1,
            grid=(db, 2 * n_pages),
            in_specs=[
                row, row, row,
                pl.BlockSpec((N_HEADS, past * N_HEADS), lambda b, s, pt: (0, 0)),
                pl.BlockSpec(page_block, k_map),
                pl.BlockSpec(page_block, v_map),
            ],
            out_specs=row,
            scratch_shapes=[
                pltpu.VMEM((N_HEADS, past * N_HEADS), F32),
                pltpu.VMEM((N_HEADS, past * N_HEADS), BF16),
                pltpu.VMEM((N_HEADS, HEAD_DIM), F32),
                pltpu.VMEM((N_HEADS, 1), F32),
                pltpu.VMEM((N_HEADS, 1), F32),
            ],
        ),
        out_shape=jax.ShapeDtypeStruct((db, N_HEADS, HEAD_DIM), F32),
        compiler_params=_params("arbitrary", "arbitrary"),
    )(page_table.reshape(-1), q, k_new, v_new, bias, cache_k, cache_v)


def _pool_counts(pos, width):
    return jnp.minimum(pos + 1, width).astype(F32)


def _pool_prompt_kernel(x_ref, sh_ref, sc_ref, o_ref, hist_ref, ext, buf_a, buf_b, *, tm, group):
    t = pl.program_id(1)
    rows = POOL_HALO + tm

    @pl.when(t == 0)
    def _():
        ext[0:POOL_HALO, :] = jnp.zeros((POOL_HALO, ext.shape[1]), F32)

    h = x_ref[...] * (1.0 + sc_ref[...]) + sh_ref[...]
    ext[POOL_HALO:rows, :] = h
    d = h.shape[1]
    buf_a[8:rows, :] = ext[8:rows, :] + ext[7:rows - 1, :]
    buf_b[16:rows, group:d] = buf_a[16:rows, group:d] + buf_a[14:rows - 2, group:d]
    buf_a[24:rows, 2 * group:d] = buf_b[24:rows, 2 * group:d] + buf_b[20:rows - 4, 2 * group:d]
    buf_b[32:rows, 3 * group:d] = buf_a[32:rows, 3 * group:d] + buf_a[24:rows - 8, 3 * group:d]
    pos = t * tm + lax.broadcasted_iota(I32, (tm, 1), 0)
    sums = (buf_a, buf_b, buf_a, buf_b)
    for g, width in enumerate(POOL_WINDOWS):
        cols = slice(g * group, (g + 1) * group)
        mean = sums[g][POOL_HALO:rows, cols] / _pool_counts(pos, width)
        o_ref[:, cols] = (mean - h[:, cols]).astype(BF16)
    ext[0:POOL_HALO, :] = h[tm - POOL_HALO:tm, :]

    @pl.when(t == pl.num_programs(1) - 1)
    def _():
        hist_ref[...] = h[tm - POOL_HIST:tm, :]


def _pool_prompt(x, mod, layer, tm):
    b, t, d = x.shape
    tok = pl.BlockSpec((None, tm, d), lambda bi, ti: (bi, ti, 0))
    return pl.pallas_call(
        functools.partial(_pool_prompt_kernel, tm=tm, group=d // len(POOL_WINDOWS)),
        grid=(b, t // tm),
        in_specs=[tok, _mod_spec(mod, layer, 0, tm), _mod_spec(mod, layer, 1, tm)],
        out_specs=[tok, pl.BlockSpec((None, POOL_HIST, d), lambda bi, ti: (bi, 0, 0))],
        out_shape=[jax.ShapeDtypeStruct((b, t, d), BF16), jax.ShapeDtypeStruct((b, POOL_HIST, d), F32)],
        scratch_shapes=[pltpu.VMEM((POOL_HALO + tm, d), F32)] * 3,
        compiler_params=_params("arbitrary", "arbitrary"),
    )(x, mod, mod)


def _pool_decode_kernel(x_ref, sh_ref, sc_ref, hist_ref, o_ref, hist_out_ref, *, group, pos0):
    h = x_ref[...] * (1.0 + sc_ref[...]) + sh_ref[...]
    rows = []
    for b in range(h.shape[0]):
        hb = h[b:b + 1, :]
        parts = []
        for g, width in enumerate(POOL_WINDOWS):
            cols = slice(g * group, (g + 1) * group)
            tail = hist_ref[b, POOL_HIST - (width - 1):POOL_HIST, cols]
            total = jnp.sum(tail, axis=0, keepdims=True) + hb[:, cols]
            parts.append(total / float(min(pos0 + 1, width)) - hb[:, cols])
        rows.append(jnp.concatenate(parts, axis=1))
        hist_out_ref[b, 0:POOL_HIST - 1, :] = hist_ref[b, 1:POOL_HIST, :]
        hist_out_ref[b, POOL_HIST - 1:POOL_HIST, :] = hb
    o_ref[...] = jnp.concatenate(rows, axis=0).astype(BF16)


def _pool_decode(x, mod, layer, hist, pos0):
    _, db, d = x.shape
    bb = 16
    assert db % bb == 0 and pos0 >= POOL_HIST
    tok = pl.BlockSpec((None, bb, d), lambda bi, ti: (0, ti, 0))
    hspec = pl.BlockSpec((bb, POOL_HIST, d), lambda bi, ti: (ti, 0, 0))
    mixed, hist_new = pl.pallas_call(
        functools.partial(_pool_decode_kernel, group=d // len(POOL_WINDOWS), pos0=pos0),
        grid=(1, db // bb),
        in_specs=[tok, _mod_spec(mod, layer, 0, bb), _mod_spec(mod, layer, 1, bb), hspec],
        out_specs=[tok, hspec],
        out_shape=[jax.ShapeDtypeStruct((1, db, d), BF16), jax.ShapeDtypeStruct((db, POOL_HIST, d), F32)],
        compiler_params=_params("arbitrary", "arbitrary"),
    )(x, mod, mod, hist)
    return mixed, hist_new


def _layer_norm(z, g, b):
    mu = jnp.mean(z, axis=-1, keepdims=True)
    zc = z - mu
    var = jnp.mean(zc * zc, axis=-1, keepdims=True)
    return zc * lax.rsqrt(var + LN_EPS) * g + b


def _first_max(vals):
    m = functools.reduce(jnp.maximum, vals)
    idx = jnp.full(m.shape, len(vals) - 1, I32)
    for i in reversed(range(len(vals) - 1)):
        idx = jnp.where(vals[i] == m, i, idx)
    return m, idx


def _top2(vals):
    m0, i0 = _first_max(vals)
    rest = [jnp.where(i0 == i, NEG_INF, v) for i, v in enumerate(vals)]
    m1, i1 = _first_max(rest)
    return m0, i0, m1, i1


def _post_kernel(o_ref, x_ref, g1_ref, sh2_ref, sc2_ref, w_ref, cs_ref, lng_ref, lnb_ref,
                 wr_hi_ref, wr_lo_ref, br_ref, xo_ref, h2_ref, rt_ref, *, alpha):
    out = _dot(o_ref[...], w_ref[...]) * cs_ref[...]
    xn = _layer_norm(alpha * x_ref[...] + g1_ref[...] * out, lng_ref[...], lnb_ref[...])
    xo_ref[...] = xn
    h2 = xn * (1.0 + sc2_ref[...]) + sh2_ref[...]
    h2_ref[...] = h2
    hi, lo = _split_bf16(h2)
    logits = _dot_nt(wr_hi_ref[...], hi) + _dot_nt(wr_hi_ref[...], lo) + _dot_nt(wr_lo_ref[...], hi) + br_ref[...]
    aff = jax.nn.sigmoid(logits)
    rows = [aff[e:e + 1, :] for e in range(N_EXPERTS)]
    gscores = []
    for g in range(N_EXPERT_GROUPS):
        m0, _, m1, _ = _top2(rows[g * EXPERTS_PER_GROUP:(g + 1) * EXPERTS_PER_GROUP])
        gscores.append(m0 + m1)
    _, g_sel = _first_max(gscores)
    in_grp = []
    for i in range(EXPERTS_PER_GROUP):
        v = rows[i]
        for g in range(1, N_EXPERT_GROUPS):
            v = jnp.where(g_sel == g, rows[g * EXPERTS_PER_GROUP + i], v)
        in_grp.append(v)
    w0, i0, w1, i1 = _top2(in_grp)
    denom = w0 + w1
    e0 = (g_sel * EXPERTS_PER_GROUP + i0).astype(F32)
    e1 = (g_sel * EXPERTS_PER_GROUP + i1).astype(F32)
    zero = jnp.zeros_like(w0)
    rt_ref[...] = jnp.concatenate([e0, e1, w0 / denom, w1 / denom, zero, zero, zero, zero], axis=0)


def _post_mixer(o, x, mod, layer, w_b, colscale, ln_g, ln_b, wr_hi, wr_lo, b_router, alpha, tm):
    b, t, d = x.shape
    tok = pl.BlockSpec((None, tm, d), lambda bi, ti: (bi, ti, 0))
    vec = pl.BlockSpec((1, d), lambda bi, ti: (0, 0))
    rtr = pl.BlockSpec((N_EXPERTS, d), lambda bi, ti: (0, 0))
    return pl.pallas_call(
        functools.partial(_post_kernel, alpha=alpha),
        grid=(b, t // tm),
        in_specs=[tok, tok, _mod_spec(mod, layer, 2, tm), _mod_spec(mod, layer, 3, tm), _mod_spec(mod, layer, 4, tm),
                  pl.BlockSpec((d, d), lambda bi, ti: (0, 0)), vec, vec, vec, rtr, rtr,
                  pl.BlockSpec((N_EXPERTS, 1), lambda bi, ti: (0, 0))],
        out_specs=[tok, tok, pl.BlockSpec((None, 8, tm), lambda bi, ti: (bi, 0, ti))],
        out_shape=[jax.ShapeDtypeStruct((b, t, d), F32), jax.ShapeDtypeStruct((b, t, d), F32),
                   jax.ShapeDtypeStruct((b, 8, t), F32)],
        compiler_params=_params("arbitrary", "arbitrary"),
    )(o, x, mod, mod, mod, w_b, colscale, ln_g, ln_b, wr_hi, wr_lo, b_router)


def _moe_kernel(be_ref, nact_ref, src_ref, dst_ref, gate_ref, h_hbm, wg_ref, wu_ref, wd_ref, y_hbm,
                xg, yb, wgb, wub, wdb, sem_in, sem_out):
    i = pl.program_id(0)
    tb = xg.shape[0]

    def gather(r):
        return pltpu.make_async_copy(h_hbm.at[pl.ds(src_ref[0, 0, r], 1), :], xg.at[pl.ds(r, 1), :], sem_in)

    def scatter(r):
        return pltpu.make_async_copy(yb.at[pl.ds(r, 1), :], y_hbm.at[pl.ds(dst_ref[0, 0, r], 1), :], sem_out)

    def each_row(fn):
        def body(r, carry):
            fn(r)
            return carry
        lax.fori_loop(0, tb, body, 0)

    @pl.when(i < nact_ref[0])
    def _():
        each_row(lambda r: gather(r).start())
        changed = jnp.logical_or(i == 0, be_ref[i] != be_ref[jnp.maximum(i - 1, 0)])

        @pl.when(changed)
        def _():
            wgb[...] = wg_ref[...].astype(BF16)
            wub[...] = wu_ref[...].astype(BF16)
            wdb[...] = wd_ref[...].astype(BF16)

        each_row(lambda r: gather(r).wait())
        x = xg[...].astype(BF16)
        a = _dot(x, wgb[...])
        a = a * jax.nn.sigmoid(a) * _dot(x, wub[...])
        yb[...] = _dot(a.astype(BF16), wdb[...]) * gate_ref[...]
        each_row(lambda r: pl.when(dst_ref[0, 0, r] >= 0)(lambda: scatter(r).start()))
        each_row(lambda r: pl.when(dst_ref[0, 0, r] >= 0)(lambda: scatter(r).wait()))


def _moe_experts(h2_flat, blk_expert, n_active, src_tok, dst_row, gate_rows, w_gate, w_up, w_down, layer, n_out_rows):
    n_rows = src_tok.shape[0]
    tb = MOE_ROWS
    n_blk = n_rows // tb
    d = h2_flat.shape[1]
    de = w_gate.shape[3]
    idx = pl.BlockSpec((1, 1, tb), lambda i, be, na: (i, 0, 0), memory_space=pltpu.SMEM)
    return pl.pallas_call(
        _moe_kernel,
        grid_spec=pltpu.PrefetchScalarGridSpec(
            num_scalar_prefetch=2,
            grid=(n_blk,),
            in_specs=[
                idx, idx,
                pl.BlockSpec((tb, 1), lambda i, be, na: (i, 0)),
                pl.BlockSpec(memory_space=pl.ANY),
                pl.BlockSpec((None, None, d, de), lambda i, be, na: (layer, be[i], 0, 0)),
                pl.BlockSpec((None, None, d, de), lambda i, be, na: (layer, be[i], 0, 0)),
                pl.BlockSpec((None, None, de, d), lambda i, be, na: (layer, be[i], 0, 0)),
            ],
            out_specs=pl.BlockSpec(memory_space=pl.ANY),
            scratch_shapes=[
                pltpu.VMEM((tb, d), F32), pltpu.VMEM((tb, d), F32),
                pltpu.VMEM((d, de), BF16), pltpu.VMEM((d, de), BF16), pltpu.VMEM((de, d), BF16),
                pltpu.SemaphoreType.DMA(()), pltpu.SemaphoreType.DMA(()),
            ],
        ),
        out_shape=jax.ShapeDtypeStruct((n_out_rows, d), F32),
        compiler_params=_params("arbitrary"),
    )(blk_expert, n_active, src_tok.reshape(n_blk, 1, tb), dst_row.reshape(n_blk, 1, tb),
      gate_rows.reshape(n_rows, 1), h2_flat, w_gate, w_up, w_down)


def _route_tables(rt, n_tok):
    tb = MOE_ROWS
    m = 2 * n_tok
    e = jnp.stack([rt[:, 0, :], rt[:, 1, :]], axis=-1).reshape(m).astype(I32)
    g = jnp.stack([rt[:, 2, :], rt[:, 3, :]], axis=-1).reshape(m)
    onehot = (e[:, None] == jnp.arange(N_EXPERTS, dtype=I32)[None, :]).astype(I32)
    counts = onehot.sum(0)
    rank = jnp.sum((jnp.cumsum(onehot, axis=0) - onehot) * onehot, axis=1)
    padded = (counts + tb - 1) // tb * tb
    pad_ends = jnp.cumsum(padded)
    pad_starts = pad_ends - padded
    dest = pad_starts[e] + rank
    n_blk = -(-(m + N_EXPERTS * (tb - 1)) // tb)
    n_rows = n_blk * tb
    pair = jnp.arange(m, dtype=I32)
    src_tok = jnp.zeros((n_rows,), I32).at[dest].set(pair // 2)
    dst_row = jnp.full((n_rows,), -1, I32).at[dest].set(pair)
    gate_rows = jnp.zeros((n_rows,), F32).at[dest].set(g)
    blk_expert = jnp.minimum(jnp.searchsorted(pad_ends, jnp.arange(n_blk, dtype=I32) * tb, side="right"),
                             N_EXPERTS - 1).astype(I32)
    n_active = (pad_ends[-1] // tb).astype(I32).reshape(1)
    return blk_expert, n_active, src_tok, dst_row, gate_rows, m


def _final_kernel(x_ref, y_ref, g2_ref, lng_ref, lnb_ref, o_ref, *, alpha, d):
    ffn = y_ref[:, 0:d] + y_ref[:, d:2 * d]
    o_ref[...] = _layer_norm(alpha * x_ref[...] + g2_ref[...] * ffn, lng_ref[...], lnb_ref[...])


def _final_ln(x, y_pairs, mod, layer, ln_g, ln_b, alpha, tm):
    b, t, d = x.shape
    tok = pl.BlockSpec((None, tm, d), lambda bi, ti: (bi, ti, 0))
    vec = pl.BlockSpec((1, d), lambda bi, ti: (0, 0))
    tiles = t // tm
    return pl.pallas_call(
        functools.partial(_final_kernel, alpha=alpha, d=d),
        grid=(b, tiles),
        in_specs=[tok, pl.BlockSpec((tm, 2 * d), lambda bi, ti: (bi * tiles + ti, 0)),
                  _mod_spec(mod, layer, 5, tm), vec, vec],
        out_specs=tok,
        out_shape=jax.ShapeDtypeStruct((b, t, d), F32),
        compiler_params=_params("arbitrary", "arbitrary"),
    )(x, y_pairs, mod, ln_g, ln_b)


def _run_trunk(x, mod, pos0, cache_k, cache_v, page_table, pool_state, prm, tm):
    b, t, d = x.shape
    depth = prm["depth"]
    alpha = (2.0 * depth) ** 0.25
    decode = cache_k is not None
    new_k, new_v, new_pool = [], [], []
    for i in range(depth):
        j = i // 2
        if i % 2 == 0:
            if decode:
                k, v, q = _qkv(x, mod, i, prm["w_qkv_b"][j], tm, prompt=False)
                per_head = (t, N_HEADS, HEAD_DIM)
                o = _moba_decode(q.astype(F32).reshape(per_head), k.reshape(per_head), v.reshape(per_head),
                                 cache_k, cache_v, j, page_table, prm["bias_decode"])
                o = o.reshape(1, t, d).astype(BF16)
            else:
                k, v, qt, vt, kh, km = _qkv(x, mod, i, prm["w_qkv_b"][j], tm, prompt=True)
                o = _moba_prompt(qt, kh, vt, km.transpose(0, 2, 1, 3), prm["bias_prompt"], prm["block_steps"])
            new_k.append(k)
            new_v.append(v)
            w_b, colscale = prm["w_o_b"][j], prm["ones"]
        else:
            if decode:
                o, hist_new = _pool_decode(x, mod, i, pool_state[j], pos0)
            else:
                o, hist_new = _pool_prompt(x, mod, i, tm)
            new_pool.append(hist_new)
            w_b, colscale = prm["w_pool_b"][j], prm["pool_scale"][j]
        x, h2, rt = _post_mixer(o, x, mod, i, w_b, colscale, prm["ln_g"][i, 0], prm["ln_b"][i, 0],
                                prm["wr_hi"], prm["wr_lo"], prm["b_router"], alpha, tm)
        n_tok = b * t
        blk_expert, n_active, src_tok, dst_row, gate_rows, n_out_rows = _route_tables(rt, n_tok)
        y = _moe_experts(h2.reshape(n_tok, d), blk_expert, n_active, src_tok, dst_row, gate_rows,
                         prm["w_gate"], prm["w_up"], prm["w_down"], i, n_out_rows)
        x = _final_ln(x, y.reshape(n_out_rows // 2, 2 * d), mod, i, prm["ln_g"][i, 1], prm["ln_b"][i, 1], alpha, tm)
    return x, new_k, new_v, new_pool


def kernel(x_prompt, x_sample, cache_k, cache_v, state_pool, page_table, c_prompt, c_sample, w_ada, b_ada, ln_g, ln_b, w_qkv, w_o, w_pool, pool_scale, w_router, b_router, w_gate, w_up, w_down):
    bp, seq, d = x_prompt.shape
    db, dec_seq, _ = x_sample.shape
    depth = w_ada.shape[0]
    assert dec_seq == 1 and d == N_HEADS * HEAD_DIM and seq % MOBA_BLOCK == 0
    pos0 = page_table.shape[1] * cache_k.shape[2]

    c_all = jnp.concatenate([c_prompt, c_sample], axis=0)
    pad = (-c_all.shape[0]) % 8
    c_all = jnp.pad(c_all, ((0, pad), (0, 0)))
    mod = _adaln(c_all, w_ada, b_ada).reshape(depth, c_all.shape[0], 6, d).transpose(0, 2, 1, 3)
    mod_p = mod[:, :, :bp].reshape(depth, 6, bp, 1, d)
    mod_s = mod[:, :, bp:bp + db].reshape(depth, 6, 1, db, d)

    group = d // len(POOL_WINDOWS)
    w_pool_dense = jnp.zeros((w_pool.shape[0], d, d), F32)
    for g in range(len(POOL_WINDOWS)):
        w_pool_dense = w_pool_dense.at[:, g * group:(g + 1) * group, g * group:(g + 1) * group].set(w_pool[:, g])
    wr_hi, wr_lo = _split_bf16(w_router.T)
    slopes2 = LOG2E * 2.0 ** (-(8.0 / N_HEADS) * jnp.arange(1, N_HEADS + 1, dtype=F32))
    kq = jnp.arange(MOBA_BLOCK, dtype=I32)
    bias_prompt = -slopes2[:, None, None] * (kq[None, None, :] - kq[None, :, None]).astype(F32)
    cache_row = jnp.arange(pos0 * N_HEADS, dtype=I32)
    bias_decode = jnp.where(cache_row[None, :] % N_HEADS == jnp.arange(N_HEADS, dtype=I32)[:, None],
                            -slopes2[:, None] * (pos0 - cache_row // N_HEADS).astype(F32)[None, :], NEG_INF)
    prm = dict(
        depth=depth,
        w_qkv_b=w_qkv.astype(BF16), w_o_b=w_o.astype(BF16), w_pool_b=w_pool_dense.astype(BF16),
        pool_scale=pool_scale.reshape(-1, 1, d), ones=jnp.ones((1, d), F32),
        ln_g=ln_g.reshape(depth, 2, 1, d), ln_b=ln_b.reshape(depth, 2, 1, d),
        wr_hi=wr_hi, wr_lo=wr_lo, b_router=b_router.reshape(N_EXPERTS, 1),
        w_gate=w_gate, w_up=w_up, w_down=w_down,
        bias_prompt=bias_prompt, block_steps=slopes2 * MOBA_BLOCK, bias_decode=bias_decode,
    )

    y_p, k_p, v_p, pool_p = _run_trunk(x_prompt, mod_p, 0, None, None, None, None, prm, tm=MOBA_BLOCK)
    y_s, k_s, v_s, pool_s = _run_trunk(
        x_sample.reshape(1, db, d), mod_s, pos0,
        cache_k, cache_v, page_table, state_pool, prm, tm=db)

    def heads(ts, lead):
        return jnp.stack(ts).reshape(len(ts), *lead, N_HEADS, HEAD_DIM)

    return (y_p, y_s.reshape(db, 1, d),
            heads(k_p, (bp, seq)), heads(v_p, (bp, seq)), jnp.stack(pool_p),
            heads(k_s, (db, 1)), heads(v_s, (db, 1)), jnp.stack(pool_s))
```

```python
import functools

import jax
import jax.numpy as jnp
from jax import lax
from jax.experimental import pallas as pl
from jax.experimental.pallas import tpu as pltpu

F32 = jnp.float32
BF16 = jnp.bfloat16
I32 = jnp.int32

N_HEADS = 16
HEAD_DIM = 64
MOBA_BLOCK = 256
MOBA_TOPK = 3
POOL_WINDOWS = (2, 4, 8, 16)
POOL_HIST = max(POOL_WINDOWS) - 1
POOL_HALO = 32
N_EXPERTS = 16
N_EXPERT_GROUPS = 4
EXPERTS_PER_GROUP = N_EXPERTS // N_EXPERT_GROUPS
LN_EPS = 1e-5
LANES = 128
ATT_HEADS = 4
ATT_LANES = ATT_HEADS * HEAD_DIM
MOE_ROWS = 256
DMA_UNROLL = 8
VMEM_LIMIT = 48 * 1024 * 1024
NEG_INF = float("-inf")
LOG2E = 1.4426950408889634
Q_PRESCALE = HEAD_DIM ** -0.5 * LOG2E


def _dot(a, b):
    return jnp.dot(a, b, preferred_element_type=F32)


def _dot_nt(a, b):
    return lax.dot_general(a, b, (((1,), (1,)), ((), ())), preferred_element_type=F32)


def _split_bf16(x):
    hi = x.astype(BF16)
    lo = (x - hi.astype(F32)).astype(BF16)
    return hi, lo


def _params(*sem):
    return pltpu.CompilerParams(dimension_semantics=sem, vmem_limit_bytes=VMEM_LIMIT)


def _adaln_kernel(c_ref, w_ref, b_ref, o_ref):
    c = c_ref[...]
    s_hi, s_lo = _split_bf16(c * jax.nn.sigmoid(c))
    w_hi, w_lo = _split_bf16(w_ref[...])
    o_ref[...] = _dot(s_hi, w_hi) + _dot(s_lo, w_hi) + _dot(s_hi, w_lo) + b_ref[...]


def _adaln(c, w_ada, b_ada):
    depth, d, d6 = w_ada.shape
    bc = c.shape[0]
    tn = 1024
    return pl.pallas_call(
        _adaln_kernel,
        grid=(depth, d6 // tn),
        in_specs=[
            pl.BlockSpec((bc, d), lambda l, j: (0, 0)),
            pl.BlockSpec((None, d, tn), lambda l, j: (l, 0, j)),
            pl.BlockSpec((None, 1, tn), lambda l, j: (l, 0, j)),
        ],
        out_specs=pl.BlockSpec((None, bc, tn), lambda l, j: (l, 0, j)),
        out_shape=jax.ShapeDtypeStruct((depth, bc, d6), F32),
        compiler_params=_params("arbitrary", "arbitrary"),
    )(c, w_ada, b_ada.reshape(depth, 1, d6))


def _mod_spec(mod, layer, which, tm):
    r = mod.shape[3]
    d = mod.shape[4]
    if r == 1:
        return pl.BlockSpec((None, None, None, 1, d), lambda b, t: (layer, which, b, 0, 0))
    return pl.BlockSpec((None, None, None, tm, d), lambda b, t: (layer, which, b, t, 0))


def _qkv_kernel(x_ref, sh_ref, sc_ref, w_ref, k_ref, v_ref, *extra_refs, d, n_kblk):
    h = x_ref[...] * (1.0 + sc_ref[...]) + sh_ref[...]
    hb = h.astype(BF16)
    q = _dot(hb, w_ref[:, 0:d]) * Q_PRESCALE
    k = _dot(hb, w_ref[:, d:2 * d])
    v = _dot(hb, w_ref[:, 2 * d:3 * d])
    k_ref[...] = k
    v_ref[...] = v
    if n_kblk == 0:
        extra_refs[0][...] = q.astype(BF16)
        return
    qt_ref, vt_ref, kh_ref, km_ref = extra_refs
    qt_ref[...] = q.T.astype(BF16)
    vt_ref[...] = v.T.astype(BF16)
    for hd in range(N_HEADS):
        cols = slice(hd * HEAD_DIM, (hd + 1) * HEAD_DIM)
        kh_ref[hd] = k[:, cols].astype(BF16)
        for i in range(n_kblk):
            km_ref[i, hd:hd + 1, :] = jnp.mean(k[i * MOBA_BLOCK:(i + 1) * MOBA_BLOCK, cols], axis=0, keepdims=True)


def _qkv(x, mod, layer, w_qkv_b, tm, prompt):
    b, t, d = x.shape
    n_kblk = tm // MOBA_BLOCK if prompt else 0
    tok = pl.BlockSpec((None, tm, d), lambda bi, ti: (bi, ti, 0))
    out_specs = [tok] * 2
    out_shape = [jax.ShapeDtypeStruct((b, t, d), F32)] * 2
    if prompt:
        tr = pl.BlockSpec((None, d, tm), lambda bi, ti: (bi, 0, ti))
        out_specs += [tr, tr, pl.BlockSpec((None, N_HEADS, tm, HEAD_DIM), lambda bi, ti: (bi, 0, ti, 0)),
                      pl.BlockSpec((None, n_kblk, N_HEADS, HEAD_DIM), lambda bi, ti: (bi, ti, 0, 0))]
        out_shape += [jax.ShapeDtypeStruct((b, d, t), BF16)] * 2
        out_shape += [jax.ShapeDtypeStruct((b, N_HEADS, t, HEAD_DIM), BF16),
                      jax.ShapeDtypeStruct((b, t // MOBA_BLOCK, N_HEADS, HEAD_DIM), F32)]
    else:
        out_specs.append(tok)
        out_shape.append(jax.ShapeDtypeStruct((b, t, d), BF16))
    return pl.pallas_call(
        functools.partial(_qkv_kernel, d=d, n_kblk=n_kblk),
        grid=(b, t // tm),
        in_specs=[tok, _mod_spec(mod, layer, 0, tm), _mod_spec(mod, layer, 1, tm),
                  pl.BlockSpec((d, 3 * d), lambda bi, ti: (0, 0))],
        out_specs=out_specs,
        out_shape=out_shape,
        compiler_params=_params("arbitrary", "arbitrary"),
    )(x, mod, mod, w_qkv_b)


def _moba_kernel(offs_ref, qt_ref, k_ref, vt_ref, km_ref, bias_ref, o_ref, sel_scr, *, n_blocks):
    hp = pl.program_id(1)
    own = pl.program_id(2)
    tq = MOBA_BLOCK
    blk = lax.broadcasted_iota(I32, (n_blocks, tq), 0)
    blk_f = blk.astype(F32)
    causal = lax.broadcasted_iota(I32, (MOBA_BLOCK, tq), 0) <= lax.broadcasted_iota(I32, (MOBA_BLOCK, tq), 1)
    own_start = pl.multiple_of(own * MOBA_BLOCK, MOBA_BLOCK)
    heads = range(ATT_HEADS)
    rows = [slice(hh * HEAD_DIM, (hh + 1) * HEAD_DIM) for hh in heads]
    block_step = [offs_ref[hp * ATT_HEADS + hh] for hh in heads]
    qt = [qt_ref[r, :] for r in rows]

    state = []
    for hh in heads:
        gate = jnp.where(blk < own, _dot(km_ref[hh].astype(BF16), qt[hh]), NEG_INF)
        sel = jnp.zeros((n_blocks, tq), F32)
        for _ in range(MOBA_TOPK):
            m = jnp.max(gate, axis=0, keepdims=True)
            first = jnp.min(jnp.where(gate == m, blk_f, float(n_blocks)), axis=0, keepdims=True)
            pick = (blk_f == first) & (m > NEG_INF)
            sel = jnp.where(pick, 1.0, sel)
            gate = jnp.where(pick, NEG_INF, gate)
        sel_scr[hh] = sel

        t = _dot(k_ref[hh, pl.ds(own_start, MOBA_BLOCK), :], qt[hh]) + bias_ref[hh]
        t = jnp.where(causal, t, NEG_INF)
        m_i = jnp.max(t, axis=0, keepdims=True)
        p = jnp.exp2(t - m_i)
        l_i = jnp.sum(p, axis=0, keepdims=True)
        acc = _dot(vt_ref[rows[hh], pl.ds(own_start, MOBA_BLOCK)], p.astype(BF16))
        state.append((m_i, l_i, acc))

    def body(n, state):
        start = pl.multiple_of(n * MOBA_BLOCK, MOBA_BLOCK)
        blocks_back = (own - n).astype(F32)
        new_state = []
        scores = [_dot(k_ref[hh, pl.ds(start, MOBA_BLOCK), :], qt[hh]) for hh in heads]
        for hh in heads:
            m_i, l_i, acc = state[hh]
            off = blocks_back * block_step[hh]
            t = scores[hh] + bias_ref[hh]
            chosen = sel_scr[hh, pl.ds(n, 1), :] > 0.0
            m_new = jnp.maximum(m_i, jnp.where(chosen, jnp.max(t, axis=0, keepdims=True) - off, NEG_INF))
            alpha = jnp.exp2(m_i - m_new)
            p = jnp.exp2(t - jnp.where(chosen, m_new + off, float("inf")))
            l_new = alpha * l_i + jnp.sum(p, axis=0, keepdims=True)
            acc_new = alpha * acc + _dot(vt_ref[rows[hh], pl.ds(start, MOBA_BLOCK)], p.astype(BF16))
            new_state.append((m_new, l_new, acc_new))
        return tuple(new_state)

    state = lax.fori_loop(0, own, body, tuple(state))
    out = jnp.concatenate([acc / l_i for _, l_i, acc in state], axis=0)
    o_ref[...] = out.T.astype(BF16)


def _moba_prompt(qt, kh, vt, kmean, bias, block_steps):
    b, d, t = qt.shape
    n_blocks = t // MOBA_BLOCK
    return pl.pallas_call(
        functools.partial(_moba_kernel, n_blocks=n_blocks),
        grid=(b, d // ATT_LANES, n_blocks),
        in_specs=[
            pl.BlockSpec(memory_space=pltpu.SMEM),
            pl.BlockSpec((None, ATT_LANES, MOBA_BLOCK), lambda bi, hp, qi: (bi, hp, qi)),
            pl.BlockSpec((None, ATT_HEADS, t, HEAD_DIM), lambda bi, hp, qi: (bi, hp, 0, 0)),
            pl.BlockSpec((None, ATT_LANES, t), lambda bi, hp, qi: (bi, hp, 0)),
            pl.BlockSpec((None, ATT_HEADS, n_blocks, HEAD_DIM), lambda bi, hp, qi: (bi, hp, 0, 0)),
            pl.BlockSpec((ATT_HEADS, MOBA_BLOCK, MOBA_BLOCK), lambda bi, hp, qi: (hp, 0, 0)),
        ],
        out_specs=pl.BlockSpec((None, MOBA_BLOCK, ATT_LANES), lambda bi, hp, qi: (bi, qi, hp)),
        out_shape=jax.ShapeDtypeStruct((b, t, d), BF16),
        scratch_shapes=[pltpu.VMEM((ATT_HEADS, n_blocks, MOBA_BLOCK), F32)],
        compiler_params=_params("arbitrary", "arbitrary", "arbitrary"),
    )(block_steps, qt, kh, vt, kmean, bias)


def _dec_attn_kernel(pt_ref, qt_ref, q_ref, kn_ref, vn_ref, bias_ref, kp_ref, vp_ref, o_ref,
                     qb_scr, s_scr, p_scr, acc_scr, pown_scr, l_scr, *, n_pages, page):
    del pt_ref
    s_id = pl.program_id(1)
    n_past_blocks = n_pages * page // MOBA_BLOCK
    q = q_ref[...]

    def lanes(n):
        return slice(n * MOBA_BLOCK, (n + 1) * MOBA_BLOCK)

    @pl.when(s_id == 0)
    def _():
        qt = qt_ref[...]
        for hd in range(N_HEADS):
            qb_scr[hd] = jnp.broadcast_to(qt[:, hd:hd + 1], (HEAD_DIM, page))

    @pl.when(s_id < n_pages)
    def _():
        rows = [jnp.sum(kp_ref[hd] * qb_scr[hd], axis=0, keepdims=True) for hd in range(N_HEADS)]
        s_scr[:, pl.ds(pl.multiple_of(s_id * page, page), page)] = jnp.concatenate(rows, axis=0)

    @pl.when(s_id == n_pages - 1)
    def _():
        gates = [jnp.sum(s_scr[:, lanes(n)], axis=1, keepdims=True) / MOBA_BLOCK for n in range(n_past_blocks)]
        sel = [jnp.zeros((N_HEADS, 1), F32) for _ in gates]
        for _ in range(MOBA_TOPK):
            m = functools.reduce(jnp.maximum, gates)
            first = jnp.full((N_HEADS, 1), n_past_blocks, I32)
            for n in reversed(range(n_past_blocks)):
                first = jnp.where(gates[n] == m, n, first)
            for n in range(n_past_blocks):
                pick = (first == n) & (m > NEG_INF)
                sel[n] = jnp.where(pick, 1.0, sel[n])
                gates[n] = jnp.where(pick, NEG_INF, gates[n])

        def logits(n):
            return jnp.where(sel[n] > 0.0, s_scr[:, lanes(n)] + bias_ref[:, lanes(n)], NEG_INF)

        s_own = jnp.sum(q * kn_ref[...], axis=1, keepdims=True)
        m = functools.reduce(jnp.maximum, [jnp.max(logits(n), axis=1, keepdims=True)
                                           for n in range(n_past_blocks)] + [s_own])
        l = jnp.exp2(s_own - m)
        pown_scr[...] = l
        for n in range(n_past_blocks):
            p = jnp.exp2(logits(n) - m)
            l = l + jnp.sum(p, axis=1, keepdims=True)
            p_scr[:, lanes(n)] = p
        l_scr[...] = l
        acc_scr[...] = jnp.zeros_like(acc_scr)

    @pl.when(s_id >= n_pages)
    def _():
        start = pl.multiple_of((s_id - n_pages) * page, page)
        for hd in range(N_HEADS):
            acc_scr[hd] += vp_ref[hd] * p_scr[hd:hd + 1, pl.ds(start, page)]

    @pl.when(s_id == 2 * n_pages - 1)
    def _():
        ones = jnp.ones((8, page), BF16)
        rows = []
        for hd in range(N_HEADS):
            hi, lo = _split_bf16(acc_scr[hd])
            rows.append((_dot_nt(ones, hi) + _dot_nt(ones, lo))[0:1, :])
        past_part = jnp.concatenate(rows, axis=0)
        o_ref[...] = (past_part + pown_scr[...] * vn_ref[...]) / l_scr[...]


def _moba_decode(q, k_new, v_new, cache_k, cache_v, att_layer, page_table, bias):
    db = q.shape[0]
    n_pages = page_table.shape[1]
    page = cache_k.shape[4]
    past = n_pages * page
    assert past % MOBA_BLOCK == 0 and MOBA_BLOCK % page == 0 and page % LANES == 0
    row = pl.BlockSpec((None, N_HEADS, HEAD_DIM), lambda b, s, pt: (b, 0, 0))
    page_block = (None, None, N_HEADS, HEAD_DIM, page)

    def k_map(b, s, pt):
        return (att_layer, pt[b * n_pages + jnp.minimum(s, n_pages - 1)], 0, 0, 0)

    def v_map(b, s, pt):
        return (att_layer, pt[b * n_pages + jnp.maximum(s - n_pages, 0)], 0, 0, 0)

    return pl.pallas_call(
        functools.partial(_dec_attn_kernel, n_pages=n_pages, page=page),
        grid_spec=pltpu.PrefetchScalarGridSpec(
            num_scalar_prefetch=1,
            grid=(db, 2 * n_pages),
            in_specs=[
                pl.BlockSpec((None, HEAD_DIM, N_HEADS), lambda b, s, pt: (b, 0, 0)),
                row, row, row,
                pl.BlockSpec((N_HEADS, past), lambda b, s, pt: (0, 0)),
                pl.BlockSpec(page_block, k_map),
                pl.BlockSpec(page_block, v_map),
            ],
            out_specs=row,
            scratch_shapes=[
                pltpu.VMEM((N_HEADS, HEAD_DIM, page), F32),
                pltpu.VMEM((N_HEADS, past), F32),
                pltpu.VMEM((N_HEADS, past), F32),
                pltpu.VMEM((N_HEADS, HEAD_DIM, page), F32),
                pltpu.VMEM((N_HEADS, 1), F32),
                pltpu.VMEM((N_HEADS, 1), F32),
            ],
        ),
        out_shape=jax.ShapeDtypeStruct((db, N_HEADS, HEAD_DIM), F32),
        compiler_params=_params("arbitrary", "arbitrary"),
    )(page_table.reshape(-1), q.transpose(0, 2, 1), q, k_new, v_new, bias, cache_k, cache_v)


def _pool_counts(pos, width):
    return jnp.minimum(pos + 1, width).astype(F32)


def _pool_prompt_kernel(x_ref, sh_ref, sc_ref, o_ref, hist_ref, ext, buf_a, buf_b, *, tm, group):
    t = pl.program_id(1)
    rows = POOL_HALO + tm

    @pl.when(t == 0)
    def _():
        ext[0:POOL_HALO, :] = jnp.zeros((POOL_HALO, ext.shape[1]), F32)

    h = x_ref[...] * (1.0 + sc_ref[...]) + sh_ref[...]
    ext[POOL_HALO:rows, :] = h
    d = h.shape[1]
    buf_a[8:rows, :] = ext[8:rows, :] + ext[7:rows - 1, :]
    buf_b[16:rows, group:d] = buf_a[16:rows, group:d] + buf_a[14:rows - 2, group:d]
    buf_a[24:rows, 2 * group:d] = buf_b[24:rows, 2 * group:d] + buf_b[20:rows - 4, 2 * group:d]
    buf_b[32:rows, 3 * group:d] = buf_a[32:rows, 3 * group:d] + buf_a[24:rows - 8, 3 * group:d]
    pos = t * tm + lax.broadcasted_iota(I32, (tm, 1), 0)
    sums = (buf_a, buf_b, buf_a, buf_b)
    for g, width in enumerate(POOL_WINDOWS):
        cols = slice(g * group, (g + 1) * group)
        mean = sums[g][POOL_HALO:rows, cols] / _pool_counts(pos, width)
        o_ref[:, cols] = (mean - h[:, cols]).astype(BF16)
    ext[0:POOL_HALO, :] = h[tm - POOL_HALO:tm, :]

    @pl.when(t == pl.num_programs(1) - 1)
    def _():
        hist_ref[...] = h[tm - POOL_HIST:tm, :]


def _pool_prompt(x, mod, layer, tm):
    b, t, d = x.shape
    tok = pl.BlockSpec((None, tm, d), lambda bi, ti: (bi, ti, 0))
    return pl.pallas_call(
        functools.partial(_pool_prompt_kernel, tm=tm, group=d // len(POOL_WINDOWS)),
        grid=(b, t // tm),
        in_specs=[tok, _mod_spec(mod, layer, 0, tm), _mod_spec(mod, layer, 1, tm)],
        out_specs=[tok, pl.BlockSpec((None, POOL_HIST, d), lambda bi, ti: (bi, 0, 0))],
        out_shape=[jax.ShapeDtypeStruct((b, t, d), BF16), jax.ShapeDtypeStruct((b, POOL_HIST, d), F32)],
        scratch_shapes=[pltpu.VMEM((POOL_HALO + tm, d), F32)] * 3,
        compiler_params=_params("arbitrary", "arbitrary"),
    )(x, mod, mod)


def _pool_decode_kernel(x_ref, sh_ref, sc_ref, hist_ref, o_ref, hist_out_ref, *, group, pos0):
    h = x_ref[...] * (1.0 + sc_ref[...]) + sh_ref[...]
    rows = []
    for b in range(h.shape[0]):
        hb = h[b:b + 1, :]
        parts = []
        for g, width in enumerate(POOL_WINDOWS):
            cols = slice(g * group, (g + 1) * group)
            tail = hist_ref[b, POOL_HIST - (width - 1):POOL_HIST, cols]
            total = jnp.sum(tail, axis=0, keepdims=True) + hb[:, cols]
            parts.append(total / float(min(pos0 + 1, width)) - hb[:, cols])
        rows.append(jnp.concatenate(parts, axis=1))
        hist_out_ref[b, 0:POOL_HIST - 1, :] = hist_ref[b, 1:POOL_HIST, :]
        hist_out_ref[b, POOL_HIST - 1:POOL_HIST, :] = hb
    o_ref[...] = jnp.concatenate(rows, axis=0).astype(BF16)


def _pool_decode(x, mod, layer, hist, pos0):
    _, db, d = x.shape
    bb = 16
    assert db % bb == 0 and pos0 >= POOL_HIST
    tok = pl.BlockSpec((None, bb, d), lambda bi, ti: (0, ti, 0))
    hspec = pl.BlockSpec((bb, POOL_HIST, d), lambda bi, ti: (ti, 0, 0))
    mixed, hist_new = pl.pallas_call(
        functools.partial(_pool_decode_kernel, group=d // len(POOL_WINDOWS), pos0=pos0),
        grid=(1, db // bb),
        in_specs=[tok, _mod_spec(mod, layer, 0, bb), _mod_spec(mod, layer, 1, bb), hspec],
        out_specs=[tok, hspec],
        out_shape=[jax.ShapeDtypeStruct((1, db, d), BF16), jax.ShapeDtypeStruct((db, POOL_HIST, d), F32)],
        compiler_params=_params("arbitrary", "arbitrary"),
    )(x, mod, mod, hist)
    return mixed, hist_new


def _layer_norm(z, g, b):
    mu = jnp.mean(z, axis=-1, keepdims=True)
    zc = z - mu
    var = jnp.mean(zc * zc, axis=-1, keepdims=True)
    return zc * lax.rsqrt(var + LN_EPS) * g + b


def _first_max(vals):
    m = functools.reduce(jnp.maximum, vals)
    idx = jnp.full(m.shape, len(vals) - 1, I32)
    for i in reversed(range(len(vals) - 1)):
        idx = jnp.where(vals[i] == m, i, idx)
    return m, idx


def _top2(vals):
    m0, i0 = _first_max(vals)
    rest = [jnp.where(i0 == i, NEG_INF, v) for i, v in enumerate(vals)]
    m1, i1 = _first_max(rest)
    return m0, i0, m1, i1


def _post_kernel(o_ref, x_ref, g1_ref, sh2_ref, sc2_ref, w_ref, cs_ref, lng_ref, lnb_ref,
                 wr_hi_ref, wr_lo_ref, br_ref, xo_ref, h2_ref, rt_ref, *, alpha):
    out = _dot(o_ref[...], w_ref[...]) * cs_ref[...]
    xn = _layer_norm(alpha * x_ref[...] + g1_ref[...] * out, lng_ref[...], lnb_ref[...])
    xo_ref[...] = xn
    h2 = xn * (1.0 + sc2_ref[...]) + sh2_ref[...]
    h2_ref[...] = h2
    hi, lo = _split_bf16(h2)
    logits = _dot_nt(wr_hi_ref[...], hi) + _dot_nt(wr_hi_ref[...], lo) + _dot_nt(wr_lo_ref[...], hi) + br_ref[...]
    aff = jax.nn.sigmoid(logits)
    rows = [aff[e:e + 1, :] for e in range(N_EXPERTS)]
    gscores = []
    for g in range(N_EXPERT_GROUPS):
        m0, _, m1, _ = _top2(rows[g * EXPERTS_PER_GROUP:(g + 1) * EXPERTS_PER_GROUP])
        gscores.append(m0 + m1)
    _, g_sel = _first_max(gscores)
    in_grp = []
    for i in range(EXPERTS_PER_GROUP):
        v = rows[i]
        for g in range(1, N_EXPERT_GROUPS):
            v = jnp.where(g_sel == g, rows[g * EXPERTS_PER_GROUP + i], v)
        in_grp.append(v)
    w0, i0, w1, i1 = _top2(in_grp)
    denom = w0 + w1
    e0 = (g_sel * EXPERTS_PER_GROUP + i0).astype(F32)
    e1 = (g_sel * EXPERTS_PER_GROUP + i1).astype(F32)
    zero = jnp.zeros_like(w0)
    rt_ref[...] = jnp.concatenate([e0, e1, w0 / denom, w1 / denom, zero, zero, zero, zero], axis=0)


def _post_mixer(o, x, mod, layer, w_b, colscale, ln_g, ln_b, wr_hi, wr_lo, b_router, alpha, tm):
    b, t, d = x.shape
    tok = pl.BlockSpec((None, tm, d), lambda bi, ti: (bi, ti, 0))
    vec = pl.BlockSpec((1, d), lambda bi, ti: (0, 0))
    rtr = pl.BlockSpec((N_EXPERTS, d), lambda bi, ti: (0, 0))
    return pl.pallas_call(
        functools.partial(_post_kernel, alpha=alpha),
        grid=(b, t // tm),
        in_specs=[tok, tok, _mod_spec(mod, layer, 2, tm), _mod_spec(mod, layer, 3, tm), _mod_spec(mod, layer, 4, tm),
                  pl.BlockSpec((d, d), lambda bi, ti: (0, 0)), vec, vec, vec, rtr, rtr,
                  pl.BlockSpec((N_EXPERTS, 1), lambda bi, ti: (0, 0))],
        out_specs=[tok, tok, pl.BlockSpec((None, 8, tm), lambda bi, ti: (bi, 0, ti))],
        out_shape=[jax.ShapeDtypeStruct((b, t, d), F32), jax.ShapeDtypeStruct((b, t, d), F32),
                   jax.ShapeDtypeStruct((b, 8, t), F32)],
        compiler_params=_params("arbitrary", "arbitrary"),
    )(o, x, mod, mod, mod, w_b, colscale, ln_g, ln_b, wr_hi, wr_lo, b_router)


def _moe_kernel(be_ref, nact_ref, nval_ref, src_ref, src_next_ref, dst_ref, gate_ref, h_hbm, wg_ref, wu_ref, wd_ref,
                y_hbm, xg, yb, wgb, wub, wdb, sem_in, sem_out):
    i = pl.program_id(0)
    tb = xg.shape[1]
    n_active = nact_ref[0]
    slot = lax.rem(i, 2)

    def start_gather(idx_ref, to_slot):
        def body(g, carry):
            for u in range(DMA_UNROLL):
                r = g * DMA_UNROLL + u
                pltpu.make_async_copy(h_hbm.at[pl.ds(idx_ref[0, 0, r], 1), :], xg.at[to_slot, pl.ds(r, 1), :],
                                      sem_in.at[to_slot]).start()
            return carry
        lax.fori_loop(0, tb // DMA_UNROLL, body, 0)

    def scatter(r):
        return pltpu.make_async_copy(yb.at[pl.ds(r, 1), :], y_hbm.at[pl.ds(dst_ref[0, 0, r], 1), :], sem_out)

    def first_rows(n, fn):
        def group(g, carry):
            for u in range(DMA_UNROLL):
                fn(g * DMA_UNROLL + u)
            return carry
        groups = n // DMA_UNROLL
        lax.fori_loop(0, groups, group, 0)

        def single(r, carry):
            fn(r)
            return carry
        lax.fori_loop(groups * DMA_UNROLL, n, single, 0)

    @pl.when(i < n_active)
    def _():
        @pl.when(i == 0)
        def _():
            start_gather(src_ref, 0)

        @pl.when(i + 1 < n_active)
        def _():
            start_gather(src_next_ref, 1 - slot)

        changed = jnp.logical_or(i == 0, be_ref[i] != be_ref[jnp.maximum(i - 1, 0)])

        @pl.when(changed)
        def _():
            wgb[...] = wg_ref[...].astype(BF16)
            wub[...] = wu_ref[...].astype(BF16)
            wdb[...] = wd_ref[...].astype(BF16)

        pltpu.make_async_copy(h_hbm.at[pl.ds(0, tb), :], xg.at[slot], sem_in.at[slot]).wait()
        x = xg[slot].astype(BF16)
        a = _dot(x, wgb[...])
        a = a * jax.nn.sigmoid(a) * _dot(x, wub[...])
        y = _dot(a.astype(BF16), wdb[...]) * gate_ref[...]

        @pl.when(i > 0)
        def _():
            first_rows(nval_ref[jnp.maximum(i - 1, 0)], lambda r: scatter(r).wait())

        yb[...] = y
        first_rows(nval_ref[i], lambda r: scatter(r).start())

        @pl.when(i == n_active - 1)
        def _():
            first_rows(nval_ref[i], lambda r: scatter(r).wait())


def _moe_experts(h2_flat, blk_expert, n_active, n_valid, src_tok, dst_row, gate_rows, w_gate, w_up, w_down, layer,
                 n_out_rows):
    n_rows = src_tok.shape[0]
    tb = MOE_ROWS
    n_blk = n_rows // tb
    d = h2_flat.shape[1]
    de = w_gate.shape[3]
    idx = pl.BlockSpec((1, 1, tb), lambda i, *_: (i, 0, 0), memory_space=pltpu.SMEM)
    idx_next = pl.BlockSpec((1, 1, tb), lambda i, *_: (jnp.minimum(i + 1, n_blk - 1), 0, 0), memory_space=pltpu.SMEM)
    src3 = src_tok.reshape(n_blk, 1, tb)
    return pl.pallas_call(
        _moe_kernel,
        grid_spec=pltpu.PrefetchScalarGridSpec(
            num_scalar_prefetch=3,
            grid=(n_blk,),
            in_specs=[
                idx, idx_next, idx,
                pl.BlockSpec((tb, 1), lambda i, *_: (i, 0)),
                pl.BlockSpec(memory_space=pl.ANY),
                pl.BlockSpec((None, None, d, de), lambda i, be, *_: (layer, be[i], 0, 0)),
                pl.BlockSpec((None, None, d, de), lambda i, be, *_: (layer, be[i], 0, 0)),
                pl.BlockSpec((None, None, de, d), lambda i, be, *_: (layer, be[i], 0, 0)),
            ],
            out_specs=pl.BlockSpec(memory_space=pl.ANY),
            scratch_shapes=[
                pltpu.VMEM((2, tb, d), F32), pltpu.VMEM((tb, d), F32),
                pltpu.VMEM((d, de), BF16), pltpu.VMEM((d, de), BF16), pltpu.VMEM((de, d), BF16),
                pltpu.SemaphoreType.DMA((2,)), pltpu.SemaphoreType.DMA(()),
            ],
        ),
        out_shape=jax.ShapeDtypeStruct((n_out_rows, d), F32),
        compiler_params=_params("arbitrary"),
    )(blk_expert, n_active, n_valid, src3, src3, dst_row.reshape(n_blk, 1, tb),
      gate_rows.reshape(n_rows, 1), h2_flat, w_gate, w_up, w_down)


def _route_tables(rt, n_tok):
    tb = MOE_ROWS
    m = 2 * n_tok
    e = jnp.stack([rt[:, 0, :], rt[:, 1, :]], axis=-1).reshape(m).astype(I32)
    g = jnp.stack([rt[:, 2, :], rt[:, 3, :]], axis=-1).reshape(m)
    onehot = (e[:, None] == jnp.arange(N_EXPERTS, dtype=I32)[None, :]).astype(I32)
    counts = onehot.sum(0)
    rank = jnp.sum((jnp.cumsum(onehot, axis=0) - onehot) * onehot, axis=1)
    padded = (counts + tb - 1) // tb * tb
    pad_ends = jnp.cumsum(padded)
    pad_starts = pad_ends - padded
    dest = pad_starts[e] + rank
    n_blk = -(-(m + N_EXPERTS * (tb - 1)) // tb)
    n_rows = n_blk * tb
    pair = jnp.arange(m, dtype=I32)
    src_tok = jnp.zeros((n_rows,), I32).at[dest].set(pair // 2)
    dst_row = jnp.zeros((n_rows,), I32).at[dest].set(pair)
    gate_rows = jnp.zeros((n_rows,), F32).at[dest].set(g)
    blk_start = jnp.arange(n_blk, dtype=I32) * tb
    blk_expert = jnp.minimum(jnp.searchsorted(pad_ends, blk_start, side="right"), N_EXPERTS - 1).astype(I32)
    n_active = (pad_ends[-1] // tb).astype(I32).reshape(1)
    n_valid = jnp.clip(counts[blk_expert] - (blk_start - pad_starts[blk_expert]), 0, tb).astype(I32)
    return blk_expert, n_active, n_valid, src_tok, dst_row, gate_rows, m


def _final_kernel(x_ref, y_ref, g2_ref, lng_ref, lnb_ref, o_ref, *, alpha, d):
    ffn = y_ref[:, 0:d] + y_ref[:, d:2 * d]
    o_ref[...] = _layer_norm(alpha * x_ref[...] + g2_ref[...] * ffn, lng_ref[...], lnb_ref[...])


def _final_ln(x, y_pairs, mod, layer, ln_g, ln_b, alpha, tm):
    b, t, d = x.shape
    tok = pl.BlockSpec((None, tm, d), lambda bi, ti: (bi, ti, 0))
    vec = pl.BlockSpec((1, d), lambda bi, ti: (0, 0))
    tiles = t // tm
    return pl.pallas_call(
        functools.partial(_final_kernel, alpha=alpha, d=d),
        grid=(b, tiles),
        in_specs=[tok, pl.BlockSpec((tm, 2 * d), lambda bi, ti: (bi * tiles + ti, 0)),
                  _mod_spec(mod, layer, 5, tm), vec, vec],
        out_specs=tok,
        out_shape=jax.ShapeDtypeStruct((b, t, d), F32),
        compiler_params=_params("arbitrary", "arbitrary"),
    )(x, y_pairs, mod, ln_g, ln_b)


def _run_trunk(x, mod, pos0, cache_k, cache_v, page_table, pool_state, prm, tm):
    b, t, d = x.shape
    depth = prm["depth"]
    alpha = (2.0 * depth) ** 0.25
    decode = cache_k is not None
    new_k, new_v, new_pool = [], [], []
    for i in range(depth):
        j = i // 2
        if i % 2 == 0:
            if decode:
                k, v, q = _qkv(x, mod, i, prm["w_qkv_b"][j], tm, prompt=False)
                per_head = (t, N_HEADS, HEAD_DIM)
                o = _moba_decode(q.astype(F32).reshape(per_head), k.reshape(per_head), v.reshape(per_head),
                                 cache_k, cache_v, j, page_table, prm["bias_decode"])
                o = o.reshape(1, t, d).astype(BF16)
            else:
                k, v, qt, vt, kh, km = _qkv(x, mod, i, prm["w_qkv_b"][j], tm, prompt=True)
                o = _moba_prompt(qt, kh, vt, km.transpose(0, 2, 1, 3), prm["bias_prompt"], prm["block_steps"])
            new_k.append(k)
            new_v.append(v)
            w_b, colscale = prm["w_o_b"][j], prm["ones"]
        else:
            if decode:
                o, hist_new = _pool_decode(x, mod, i, pool_state[j], pos0)
            else:
                o, hist_new = _pool_prompt(x, mod, i, tm)
            new_pool.append(hist_new)
            w_b, colscale = prm["w_pool_b"][j], prm["pool_scale"][j]
        x, h2, rt = _post_mixer(o, x, mod, i, w_b, colscale, prm["ln_g"][i, 0], prm["ln_b"][i, 0],
                                prm["wr_hi"], prm["wr_lo"], prm["b_router"], alpha, tm)
        n_tok = b * t
        blk_expert, n_active, n_valid, src_tok, dst_row, gate_rows, n_out_rows = _route_tables(rt, n_tok)
        y = _moe_experts(h2.reshape(n_tok, d), blk_expert, n_active, n_valid, src_tok, dst_row, gate_rows,
                         prm["w_gate"], prm["w_up"], prm["w_down"], i, n_out_rows)
        x = _final_ln(x, y.reshape(n_out_rows // 2, 2 * d), mod, i, prm["ln_g"][i, 1], prm["ln_b"][i, 1], alpha, tm)
    return x, new_k, new_v, new_pool


def kernel(x_prompt, x_sample, cache_k, cache_v, state_pool, page_table, c_prompt, c_sample, w_ada, b_ada, ln_g, ln_b, w_qkv, w_o, w_pool, pool_scale, w_router, b_router, w_gate, w_up, w_down):
    bp, seq, d = x_prompt.shape
    db, dec_seq, _ = x_sample.shape
    depth = w_ada.shape[0]
    assert dec_seq == 1 and d == N_HEADS * HEAD_DIM and seq % MOBA_BLOCK == 0
    pos0 = page_table.shape[1] * cache_k.shape[2]

    c_all = jnp.concatenate([c_prompt, c_sample], axis=0)
    pad = (-c_all.shape[0]) % 8
    c_all = jnp.pad(c_all, ((0, pad), (0, 0)))
    mod = _adaln(c_all, w_ada, b_ada).reshape(depth, c_all.shape[0], 6, d).transpose(0, 2, 1, 3)
    mod_p = mod[:, :, :bp].reshape(depth, 6, bp, 1, d)
    mod_s = mod[:, :, bp:bp + db].reshape(depth, 6, 1, db, d)

    group = d // len(POOL_WINDOWS)
    w_pool_dense = jnp.zeros((w_pool.shape[0], d, d), F32)
    for g in range(len(POOL_WINDOWS)):
        w_pool_dense = w_pool_dense.at[:, g * group:(g + 1) * group, g * group:(g + 1) * group].set(w_pool[:, g])
    wr_hi, wr_lo = _split_bf16(w_router.T)
    slopes2 = LOG2E * 2.0 ** (-(8.0 / N_HEADS) * jnp.arange(1, N_HEADS + 1, dtype=F32))
    kq = jnp.arange(MOBA_BLOCK, dtype=I32)
    bias_prompt = -slopes2[:, None, None] * (kq[None, None, :] - kq[None, :, None]).astype(F32)
    bias_decode = -slopes2[:, None] * (pos0 - jnp.arange(pos0, dtype=I32)).astype(F32)[None, :]
    cache_kt = cache_k.transpose(0, 1, 3, 4, 2)
    cache_vt = cache_v.transpose(0, 1, 3, 4, 2)
    prm = dict(
        depth=depth,
        w_qkv_b=w_qkv.astype(BF16), w_o_b=w_o.astype(BF16), w_pool_b=w_pool_dense.astype(BF16),
        pool_scale=pool_scale.reshape(-1, 1, d), ones=jnp.ones((1, d), F32),
        ln_g=ln_g.reshape(depth, 2, 1, d), ln_b=ln_b.reshape(depth, 2, 1, d),
        wr_hi=wr_hi, wr_lo=wr_lo, b_router=b_router.reshape(N_EXPERTS, 1),
        w_gate=w_gate, w_up=w_up, w_down=w_down,
        bias_prompt=bias_prompt, block_steps=slopes2 * MOBA_BLOCK, bias_decode=bias_decode,
    )

    y_p, k_p, v_p, pool_p = _run_trunk(x_prompt, mod_p, 0, None, None, None, None, prm, tm=MOBA_BLOCK)
    y_s, k_s, v_s, pool_s = _run_trunk(
        x_sample.reshape(1, db, d), mod_s, pos0,
        cache_kt, cache_vt, page_table, state_pool, prm, tm=db)

    def heads(ts, lead):
        return jnp.stack(ts).reshape(len(ts), *lead, N_HEADS, HEAD_DIM)

    return (y_p, y_s.reshape(db, 1, d),
            heads(k_p, (bp, seq)), heads(v_p, (bp, seq)), jnp.stack(pool_p),
            heads(k_s, (db, 1)), heads(v_s, (db, 1)), jnp.stack(pool_s))
```

```python
import functools

import jax
import jax.numpy as jnp
from jax import lax
from jax.experimental import pallas as pl
from jax.experimental.pallas import tpu as pltpu

F32 = jnp.float32
BF16 = jnp.bfloat16
I32 = jnp.int32

N_HEADS = 16
HEAD_DIM = 64
MOBA_BLOCK = 256
MOBA_TOPK = 3
POOL_WINDOWS = (2, 4, 8, 16)
POOL_HIST = max(POOL_WINDOWS) - 1
POOL_HALO = 32
N_EXPERTS = 16
N_EXPERT_GROUPS = 4
EXPERTS_PER_GROUP = N_EXPERTS // N_EXPERT_GROUPS
LN_EPS = 1e-5
LANES = 128
ATT_HEADS = 4
ATT_LANES = ATT_HEADS * HEAD_DIM
MOE_ROWS = 256
DMA_UNROLL = 8
DECODE_PAGES_PER_STEP = 8
VMEM_LIMIT = 48 * 1024 * 1024
NEG_INF = float("-inf")
LOG2E = 1.4426950408889634
Q_PRESCALE = HEAD_DIM ** -0.5 * LOG2E


def _dot(a, b):
    return jnp.dot(a, b, preferred_element_type=F32)


def _dot_nt(a, b):
    return lax.dot_general(a, b, (((1,), (1,)), ((), ())), preferred_element_type=F32)


def _split_bf16(x):
    hi = x.astype(BF16)
    lo = (x - hi.astype(F32)).astype(BF16)
    return hi, lo


def _params(*sem):
    return pltpu.CompilerParams(dimension_semantics=sem, vmem_limit_bytes=VMEM_LIMIT)


def _adaln_kernel(c_ref, w_ref, b_ref, o_ref):
    c = c_ref[...]
    s_hi, s_lo = _split_bf16(c * jax.nn.sigmoid(c))
    w_hi, w_lo = _split_bf16(w_ref[...])
    o_ref[...] = _dot(s_hi, w_hi) + _dot(s_lo, w_hi) + _dot(s_hi, w_lo) + b_ref[...]


def _adaln(c, w_ada, b_ada):
    depth, d, d6 = w_ada.shape
    bc = c.shape[0]
    tn = 1024
    return pl.pallas_call(
        _adaln_kernel,
        grid=(depth, d6 // tn),
        in_specs=[
            pl.BlockSpec((bc, d), lambda l, j: (0, 0)),
            pl.BlockSpec((None, d, tn), lambda l, j: (l, 0, j)),
            pl.BlockSpec((None, 1, tn), lambda l, j: (l, 0, j)),
        ],
        out_specs=pl.BlockSpec((None, bc, tn), lambda l, j: (l, 0, j)),
        out_shape=jax.ShapeDtypeStruct((depth, bc, d6), F32),
        compiler_params=_params("arbitrary", "arbitrary"),
    )(c, w_ada, b_ada.reshape(depth, 1, d6))


def _mod_spec(mod, layer, which, tm):
    r = mod.shape[3]
    d = mod.shape[4]
    if r == 1:
        return pl.BlockSpec((None, None, None, 1, d), lambda b, t: (layer, which, b, 0, 0))
    return pl.BlockSpec((None, None, None, tm, d), lambda b, t: (layer, which, b, t, 0))


def _qkv_kernel(x_ref, sh_ref, sc_ref, w_ref, k_ref, v_ref, *extra_refs, d, n_kblk):
    h = x_ref[...] * (1.0 + sc_ref[...]) + sh_ref[...]
    hb = h.astype(BF16)
    q = _dot(hb, w_ref[:, 0:d]) * Q_PRESCALE
    k = _dot(hb, w_ref[:, d:2 * d])
    v = _dot(hb, w_ref[:, 2 * d:3 * d])
    k_ref[...] = k
    v_ref[...] = v
    if n_kblk == 0:
        extra_refs[0][...] = q.astype(BF16)
        return
    qt_ref, vt_ref, kh_ref, km_ref = extra_refs
    qt_ref[...] = q.T.astype(BF16)
    vt_ref[...] = v.T.astype(BF16)
    for hd in range(N_HEADS):
        cols = slice(hd * HEAD_DIM, (hd + 1) * HEAD_DIM)
        kh_ref[hd] = k[:, cols].astype(BF16)
        for i in range(n_kblk):
            km_ref[i, hd:hd + 1, :] = jnp.mean(k[i * MOBA_BLOCK:(i + 1) * MOBA_BLOCK, cols], axis=0, keepdims=True)


def _qkv(x, mod, layer, w_qkv_b, tm, prompt):
    b, t, d = x.shape
    n_kblk = tm // MOBA_BLOCK if prompt else 0
    tok = pl.BlockSpec((None, tm, d), lambda bi, ti: (bi, ti, 0))
    out_specs = [tok] * 2
    out_shape = [jax.ShapeDtypeStruct((b, t, d), F32)] * 2
    if prompt:
        tr = pl.BlockSpec((None, d, tm), lambda bi, ti: (bi, 0, ti))
        out_specs += [tr, tr, pl.BlockSpec((None, N_HEADS, tm, HEAD_DIM), lambda bi, ti: (bi, 0, ti, 0)),
                      pl.BlockSpec((None, n_kblk, N_HEADS, HEAD_DIM), lambda bi, ti: (bi, ti, 0, 0))]
        out_shape += [jax.ShapeDtypeStruct((b, d, t), BF16)] * 2
        out_shape += [jax.ShapeDtypeStruct((b, N_HEADS, t, HEAD_DIM), BF16),
                      jax.ShapeDtypeStruct((b, t // MOBA_BLOCK, N_HEADS, HEAD_DIM), F32)]
    else:
        out_specs.append(tok)
        out_shape.append(jax.ShapeDtypeStruct((b, t, d), BF16))
    return pl.pallas_call(
        functools.partial(_qkv_kernel, d=d, n_kblk=n_kblk),
        grid=(b, t // tm),
        in_specs=[tok, _mod_spec(mod, layer, 0, tm), _mod_spec(mod, layer, 1, tm),
                  pl.BlockSpec((d, 3 * d), lambda bi, ti: (0, 0))],
        out_specs=out_specs,
        out_shape=out_shape,
        compiler_params=_params("arbitrary", "arbitrary"),
    )(x, mod, mod, w_qkv_b)


def _moba_kernel(offs_ref, qt_ref, k_ref, vt_ref, km_ref, bias_ref, o_ref, sel_scr, score_scr, *, n_blocks):
    hp = pl.program_id(1)
    own = pl.program_id(2)
    tq = MOBA_BLOCK
    blk = lax.broadcasted_iota(I32, (n_blocks, tq), 0)
    blk_f = blk.astype(F32)
    causal = lax.broadcasted_iota(I32, (MOBA_BLOCK, tq), 0) <= lax.broadcasted_iota(I32, (MOBA_BLOCK, tq), 1)
    own_start = pl.multiple_of(own * MOBA_BLOCK, MOBA_BLOCK)
    heads = range(ATT_HEADS)
    rows = [slice(hh * HEAD_DIM, (hh + 1) * HEAD_DIM) for hh in heads]
    block_step = [offs_ref[hp * ATT_HEADS + hh] for hh in heads]
    qt = [qt_ref[r, :] for r in rows]

    state = []
    for hh in heads:
        gate = jnp.where(blk < own, _dot(km_ref[hh].astype(BF16), qt[hh]), NEG_INF)
        sel = jnp.zeros((n_blocks, tq), F32)
        for _ in range(MOBA_TOPK):
            m = jnp.max(gate, axis=0, keepdims=True)
            first = jnp.min(jnp.where(gate == m, blk_f, float(n_blocks)), axis=0, keepdims=True)
            pick = (blk_f == first) & (m > NEG_INF)
            sel = jnp.where(pick, 1.0, sel)
            gate = jnp.where(pick, NEG_INF, gate)
        sel_scr[hh] = sel

        t = _dot(k_ref[hh, pl.ds(own_start, MOBA_BLOCK), :], qt[hh]) + bias_ref[hh]
        t = jnp.where(causal, t, NEG_INF)
        m_i = jnp.max(t, axis=0, keepdims=True)
        p = jnp.exp2(t - m_i)
        l_i = jnp.sum(p, axis=0, keepdims=True)
        acc = _dot(vt_ref[rows[hh], pl.ds(own_start, MOBA_BLOCK)], p.astype(BF16))
        state.append((m_i, l_i, acc))

    def scores_into(half, n):
        start = pl.multiple_of(jnp.minimum(n, n_blocks - 1) * MOBA_BLOCK, MOBA_BLOCK)
        for hh in heads:
            score_scr[half, hh] = _dot(k_ref[hh, pl.ds(start, MOBA_BLOCK), :], qt[hh])

    def consume(half, n, state):
        start = pl.multiple_of(jnp.minimum(n, n_blocks - 1) * MOBA_BLOCK, MOBA_BLOCK)
        blocks_back = (own - n).astype(F32)
        new_state = []
        for hh in heads:
            m_i, l_i, acc = state[hh]
            off = blocks_back * block_step[hh]
            t = score_scr[half, hh] + bias_ref[hh]
            chosen = sel_scr[hh, pl.ds(jnp.minimum(n, n_blocks - 1), 1), :] > 0.0
            m_new = jnp.maximum(m_i, jnp.where(chosen, jnp.max(t, axis=0, keepdims=True) - off, NEG_INF))
            alpha = jnp.exp2(m_i - m_new)
            p = jnp.exp2(t - jnp.where(chosen, m_new + off, float("inf")))
            l_new = alpha * l_i + jnp.sum(p, axis=0, keepdims=True)
            acc_new = alpha * acc + _dot(vt_ref[rows[hh], pl.ds(start, MOBA_BLOCK)], p.astype(BF16))
            new_state.append((m_new, l_new, acc_new))
        return tuple(new_state)

    def body(j, state):
        n = 2 * j
        scores_into(1, n + 1)
        state = consume(0, n, state)
        scores_into(0, n + 2)
        return consume(1, n + 1, state)

    scores_into(0, 0)
    state = lax.fori_loop(0, (own + 1) // 2, body, tuple(state))
    out = jnp.concatenate([acc / l_i for _, l_i, acc in state], axis=0)
    o_ref[...] = out.T.astype(BF16)


def _moba_prompt(qt, kh, vt, kmean, bias, block_steps):
    b, d, t = qt.shape
    n_blocks = t // MOBA_BLOCK
    return pl.pallas_call(
        functools.partial(_moba_kernel, n_blocks=n_blocks),
        grid=(b, d // ATT_LANES, n_blocks),
        in_specs=[
            pl.BlockSpec(memory_space=pltpu.SMEM),
            pl.BlockSpec((None, ATT_LANES, MOBA_BLOCK), lambda bi, hp, qi: (bi, hp, qi)),
            pl.BlockSpec((None, ATT_HEADS, t, HEAD_DIM), lambda bi, hp, qi: (bi, hp, 0, 0)),
            pl.BlockSpec((None, ATT_LANES, t), lambda bi, hp, qi: (bi, hp, 0)),
            pl.BlockSpec((None, ATT_HEADS, n_blocks, HEAD_DIM), lambda bi, hp, qi: (bi, hp, 0, 0)),
            pl.BlockSpec((ATT_HEADS, MOBA_BLOCK, MOBA_BLOCK), lambda bi, hp, qi: (hp, 0, 0)),
        ],
        out_specs=pl.BlockSpec((None, MOBA_BLOCK, ATT_LANES), lambda bi, hp, qi: (bi, qi, hp)),
        out_shape=jax.ShapeDtypeStruct((b, t, d), BF16),
        scratch_shapes=[pltpu.VMEM((ATT_HEADS, n_blocks, MOBA_BLOCK), F32),
                        pltpu.VMEM((2, ATT_HEADS, MOBA_BLOCK, MOBA_BLOCK), F32)],
        compiler_params=_params("arbitrary", "arbitrary", "arbitrary"),
    )(block_steps, qt, kh, vt, kmean, bias)


def _dec_attn_kernel(pt_ref, qt_ref, q_ref, kn_ref, vn_ref, bias_ref, *refs, n_pages, page, pps):
    del pt_ref
    kp_refs, vp_refs = refs[:pps], refs[pps:2 * pps]
    o_ref, qb_scr, s_scr, p_scr, acc_scr, pown_scr, l_scr = refs[2 * pps:]
    s_id = pl.program_id(1)
    k_steps = n_pages // pps
    n_past_blocks = n_pages * page // MOBA_BLOCK
    q = q_ref[...]

    def lanes(n):
        return slice(n * MOBA_BLOCK, (n + 1) * MOBA_BLOCK)

    @pl.when(s_id == 0)
    def _():
        qt = qt_ref[...]
        for hd in range(N_HEADS):
            qb_scr[hd] = jnp.broadcast_to(qt[:, hd:hd + 1], (HEAD_DIM, page))

    @pl.when(s_id < k_steps)
    def _():
        for u, kp_ref in enumerate(kp_refs):
            rows = [jnp.sum(kp_ref[hd] * qb_scr[hd], axis=0, keepdims=True) for hd in range(N_HEADS)]
            start = pl.multiple_of((s_id * pps + u) * page, page)
            s_scr[:, pl.ds(start, page)] = jnp.concatenate(rows, axis=0)

    @pl.when(s_id == k_steps - 1)
    def _():
        gates = [jnp.sum(s_scr[:, lanes(n)], axis=1, keepdims=True) / MOBA_BLOCK for n in range(n_past_blocks)]
        sel = [jnp.zeros((N_HEADS, 1), F32) for _ in gates]
        for _ in range(MOBA_TOPK):
            m = functools.reduce(jnp.maximum, gates)
            first = jnp.full((N_HEADS, 1), n_past_blocks, I32)
            for n in reversed(range(n_past_blocks)):
                first = jnp.where(gates[n] == m, n, first)
            for n in range(n_past_blocks):
                pick = (first == n) & (m > NEG_INF)
                sel[n] = jnp.where(pick, 1.0, sel[n])
                gates[n] = jnp.where(pick, NEG_INF, gates[n])

        def logits(n):
            return jnp.where(sel[n] > 0.0, s_scr[:, lanes(n)] + bias_ref[:, lanes(n)], NEG_INF)

        s_own = jnp.sum(q * kn_ref[...], axis=1, keepdims=True)
        m = functools.reduce(jnp.maximum, [jnp.max(logits(n), axis=1, keepdims=True)
                                           for n in range(n_past_blocks)] + [s_own])
        l = jnp.exp2(s_own - m)
        pown_scr[...] = l
        for n in range(n_past_blocks):
            p = jnp.exp2(logits(n) - m)
            l = l + jnp.sum(p, axis=1, keepdims=True)
            p_scr[:, lanes(n)] = p
        l_scr[...] = l
        acc_scr[...] = jnp.zeros_like(acc_scr)

    @pl.when(s_id >= k_steps)
    def _():
        for hd in range(N_HEADS):
            part = acc_scr[hd]
            for u, vp_ref in enumerate(vp_refs):
                start = pl.multiple_of(((s_id - k_steps) * pps + u) * page, page)
                part = part + vp_ref[hd] * p_scr[hd:hd + 1, pl.ds(start, page)]
            acc_scr[hd] = part

    @pl.when(s_id == 2 * k_steps - 1)
    def _():
        ones = jnp.ones((8, page), BF16)
        rows = []
        for hd in range(N_HEADS):
            hi, lo = _split_bf16(acc_scr[hd])
            rows.append((_dot_nt(ones, hi) + _dot_nt(ones, lo))[0:1, :])
        past_part = jnp.concatenate(rows, axis=0)
        o_ref[...] = (past_part + pown_scr[...] * vn_ref[...]) / l_scr[...]


def _moba_decode(q, k_new, v_new, cache_k, cache_v, att_layer, page_table, bias):
    db = q.shape[0]
    n_pages = page_table.shape[1]
    page = cache_k.shape[4]
    past = n_pages * page
    assert past % MOBA_BLOCK == 0 and MOBA_BLOCK % page == 0 and page % LANES == 0
    row = pl.BlockSpec((None, N_HEADS, HEAD_DIM), lambda b, s, pt: (b, 0, 0))
    page_block = (None, None, N_HEADS, HEAD_DIM, page)

    pps = max(u for u in range(1, DECODE_PAGES_PER_STEP + 1) if n_pages % u == 0)
    k_steps = n_pages // pps

    def k_spec(u):
        return pl.BlockSpec(page_block, lambda b, s, pt: (
            att_layer, pt[b * n_pages + jnp.minimum(s, k_steps - 1) * pps + u], 0, 0, 0))

    def v_spec(u):
        return pl.BlockSpec(page_block, lambda b, s, pt: (
            att_layer, pt[b * n_pages + jnp.maximum(s - k_steps, 0) * pps + u], 0, 0, 0))

    return pl.pallas_call(
        functools.partial(_dec_attn_kernel, n_pages=n_pages, page=page, pps=pps),
        grid_spec=pltpu.PrefetchScalarGridSpec(
            num_scalar_prefetch=1,
            grid=(db, 2 * k_steps),
            in_specs=[
                pl.BlockSpec((None, HEAD_DIM, N_HEADS), lambda b, s, pt: (b, 0, 0)),
                row, row, row,
                pl.BlockSpec((N_HEADS, past), lambda b, s, pt: (0, 0)),
                *[k_spec(u) for u in range(pps)],
                *[v_spec(u) for u in range(pps)],
            ],
            out_specs=row,
            scratch_shapes=[
                pltpu.VMEM((N_HEADS, HEAD_DIM, page), F32),
                pltpu.VMEM((N_HEADS, past), F32),
                pltpu.VMEM((N_HEADS, past), F32),
                pltpu.VMEM((N_HEADS, HEAD_DIM, page), F32),
                pltpu.VMEM((N_HEADS, 1), F32),
                pltpu.VMEM((N_HEADS, 1), F32),
            ],
        ),
        out_shape=jax.ShapeDtypeStruct((db, N_HEADS, HEAD_DIM), F32),
        compiler_params=_params("arbitrary", "arbitrary"),
    )(page_table.reshape(-1), q.transpose(0, 2, 1), q, k_new, v_new, bias, *([cache_k] * pps), *([cache_v] * pps))


def _pool_counts(pos, width):
    return jnp.minimum(pos + 1, width).astype(F32)


def _pool_prompt_kernel(x_ref, sh_ref, sc_ref, o_ref, hist_ref, ext, buf_a, buf_b, *, tm, group):
    t = pl.program_id(1)
    rows = POOL_HALO + tm

    @pl.when(t == 0)
    def _():
        ext[0:POOL_HALO, :] = jnp.zeros((POOL_HALO, ext.shape[1]), F32)

    h = x_ref[...] * (1.0 + sc_ref[...]) + sh_ref[...]
    ext[POOL_HALO:rows, :] = h
    d = h.shape[1]
    buf_a[8:rows, :] = ext[8:rows, :] + ext[7:rows - 1, :]
    buf_b[16:rows, group:d] = buf_a[16:rows, group:d] + buf_a[14:rows - 2, group:d]
    buf_a[24:rows, 2 * group:d] = buf_b[24:rows, 2 * group:d] + buf_b[20:rows - 4, 2 * group:d]
    buf_b[32:rows, 3 * group:d] = buf_a[32:rows, 3 * group:d] + buf_a[24:rows - 8, 3 * group:d]
    pos = t * tm + lax.broadcasted_iota(I32, (tm, 1), 0)
    sums = (buf_a, buf_b, buf_a, buf_b)
    for g, width in enumerate(POOL_WINDOWS):
        cols = slice(g * group, (g + 1) * group)
        mean = sums[g][POOL_HALO:rows, cols] / _pool_counts(pos, width)
        o_ref[:, cols] = (mean - h[:, cols]).astype(BF16)
    ext[0:POOL_HALO, :] = h[tm - POOL_HALO:tm, :]

    @pl.when(t == pl.num_programs(1) - 1)
    def _():
        hist_ref[...] = h[tm - POOL_HIST:tm, :]


def _pool_prompt(x, mod, layer, tm):
    b, t, d = x.shape
    tok = pl.BlockSpec((None, tm, d), lambda bi, ti: (bi, ti, 0))
    return pl.pallas_call(
        functools.partial(_pool_prompt_kernel, tm=tm, group=d // len(POOL_WINDOWS)),
        grid=(b, t // tm),
        in_specs=[tok, _mod_spec(mod, layer, 0, tm), _mod_spec(mod, layer, 1, tm)],
        out_specs=[tok, pl.BlockSpec((None, POOL_HIST, d), lambda bi, ti: (bi, 0, 0))],
        out_shape=[jax.ShapeDtypeStruct((b, t, d), BF16), jax.ShapeDtypeStruct((b, POOL_HIST, d), F32)],
        scratch_shapes=[pltpu.VMEM((POOL_HALO + tm, d), F32)] * 3,
        compiler_params=_params("arbitrary", "arbitrary"),
    )(x, mod, mod)


def _pool_decode_kernel(x_ref, sh_ref, sc_ref, hist_ref, o_ref, hist_out_ref, *, group, pos0):
    h = x_ref[...] * (1.0 + sc_ref[...]) + sh_ref[...]
    rows = []
    for b in range(h.shape[0]):
        hb = h[b:b + 1, :]
        parts = []
        for g, width in enumerate(POOL_WINDOWS):
            cols = slice(g * group, (g + 1) * group)
            tail = hist_ref[b, POOL_HIST - (width - 1):POOL_HIST, cols]
            total = jnp.sum(tail, axis=0, keepdims=True) + hb[:, cols]
            parts.append(total / float(min(pos0 + 1, width)) - hb[:, cols])
        rows.append(jnp.concatenate(parts, axis=1))
        hist_out_ref[b, 0:POOL_HIST - 1, :] = hist_ref[b, 1:POOL_HIST, :]
        hist_out_ref[b, POOL_HIST - 1:POOL_HIST, :] = hb
    o_ref[...] = jnp.concatenate(rows, axis=0).astype(BF16)


def _pool_decode(x, mod, layer, hist, pos0):
    _, db, d = x.shape
    bb = 16
    assert db % bb == 0 and pos0 >= POOL_HIST
    tok = pl.BlockSpec((None, bb, d), lambda bi, ti: (0, ti, 0))
    hspec = pl.BlockSpec((bb, POOL_HIST, d), lambda bi, ti: (ti, 0, 0))
    mixed, hist_new = pl.pallas_call(
        functools.partial(_pool_decode_kernel, group=d // len(POOL_WINDOWS), pos0=pos0),
        grid=(1, db // bb),
        in_specs=[tok, _mod_spec(mod, layer, 0, bb), _mod_spec(mod, layer, 1, bb), hspec],
        out_specs=[tok, hspec],
        out_shape=[jax.ShapeDtypeStruct((1, db, d), BF16), jax.ShapeDtypeStruct((db, POOL_HIST, d), F32)],
        compiler_params=_params("arbitrary", "arbitrary"),
    )(x, mod, mod, hist)
    return mixed, hist_new


def _layer_norm(z, g, b):
    mu = jnp.mean(z, axis=-1, keepdims=True)
    zc = z - mu
    var = jnp.mean(zc * zc, axis=-1, keepdims=True)
    return zc * lax.rsqrt(var + LN_EPS) * g + b


def _first_max(vals):
    m = functools.reduce(jnp.maximum, vals)
    idx = jnp.full(m.shape, len(vals) - 1, I32)
    for i in reversed(range(len(vals) - 1)):
        idx = jnp.where(vals[i] == m, i, idx)
    return m, idx


def _top2(vals):
    m0, i0 = _first_max(vals)
    rest = [jnp.where(i0 == i, NEG_INF, v) for i, v in enumerate(vals)]
    m1, i1 = _first_max(rest)
    return m0, i0, m1, i1


def _post_kernel(o_ref, x_ref, g1_ref, sh2_ref, sc2_ref, w_ref, cs_ref, lng_ref, lnb_ref,
                 wr_hi_ref, wr_lo_ref, br_ref, xo_ref, h2_ref, rt_ref, *, alpha):
    out = _dot(o_ref[...], w_ref[...]) * cs_ref[...]
    xn = _layer_norm(alpha * x_ref[...] + g1_ref[...] * out, lng_ref[...], lnb_ref[...])
    xo_ref[...] = xn
    h2 = xn * (1.0 + sc2_ref[...]) + sh2_ref[...]
    h2_ref[...] = h2
    hi, lo = _split_bf16(h2)
    logits = _dot_nt(wr_hi_ref[...], hi) + _dot_nt(wr_hi_ref[...], lo) + _dot_nt(wr_lo_ref[...], hi) + br_ref[...]
    aff = jax.nn.sigmoid(logits)
    rows = [aff[e:e + 1, :] for e in range(N_EXPERTS)]
    gscores = []
    for g in range(N_EXPERT_GROUPS):
        m0, _, m1, _ = _top2(rows[g * EXPERTS_PER_GROUP:(g + 1) * EXPERTS_PER_GROUP])
        gscores.append(m0 + m1)
    _, g_sel = _first_max(gscores)
    in_grp = []
    for i in range(EXPERTS_PER_GROUP):
        v = rows[i]
        for g in range(1, N_EXPERT_GROUPS):
            v = jnp.where(g_sel == g, rows[g * EXPERTS_PER_GROUP + i], v)
        in_grp.append(v)
    w0, i0, w1, i1 = _top2(in_grp)
    denom = w0 + w1
    e0 = (g_sel * EXPERTS_PER_GROUP + i0).astype(F32)
    e1 = (g_sel * EXPERTS_PER_GROUP + i1).astype(F32)
    zero = jnp.zeros_like(w0)
    rt_ref[...] = jnp.concatenate([e0, e1, w0 / denom, w1 / denom, zero, zero, zero, zero], axis=0)


def _post_mixer(o, x, mod, layer, w_b, colscale, ln_g, ln_b, wr_hi, wr_lo, b_router, alpha, tm):
    b, t, d = x.shape
    tok = pl.BlockSpec((None, tm, d), lambda bi, ti: (bi, ti, 0))
    vec = pl.BlockSpec((1, d), lambda bi, ti: (0, 0))
    rtr = pl.BlockSpec((N_EXPERTS, d), lambda bi, ti: (0, 0))
    return pl.pallas_call(
        functools.partial(_post_kernel, alpha=alpha),
        grid=(b, t // tm),
        in_specs=[tok, tok, _mod_spec(mod, layer, 2, tm), _mod_spec(mod, layer, 3, tm), _mod_spec(mod, layer, 4, tm),
                  pl.BlockSpec((d, d), lambda bi, ti: (0, 0)), vec, vec, vec, rtr, rtr,
                  pl.BlockSpec((N_EXPERTS, 1), lambda bi, ti: (0, 0))],
        out_specs=[tok, tok, pl.BlockSpec((None, 8, tm), lambda bi, ti: (bi, 0, ti))],
        out_shape=[jax.ShapeDtypeStruct((b, t, d), F32), jax.ShapeDtypeStruct((b, t, d), F32),
                   jax.ShapeDtypeStruct((b, 8, t), F32)],
        compiler_params=_params("arbitrary", "arbitrary"),
    )(o, x, mod, mod, mod, w_b, colscale, ln_g, ln_b, wr_hi, wr_lo, b_router)


def _moe_kernel(be_ref, nact_ref, nval_ref, src_ref, src_next_ref, dst_ref, gate_ref, h_hbm, wg_ref, wu_ref, wd_ref,
                y_hbm, xg, yb, wgb, wub, wdb, sem_in, sem_out):
    i = pl.program_id(0)
    tb = xg.shape[1]
    n_active = nact_ref[0]
    slot = lax.rem(i, 2)

    def start_gather(idx_ref, to_slot):
        def body(g, carry):
            for u in range(DMA_UNROLL):
                r = g * DMA_UNROLL + u
                pltpu.make_async_copy(h_hbm.at[pl.ds(idx_ref[0, 0, r], 1), :], xg.at[to_slot, pl.ds(r, 1), :],
                                      sem_in.at[to_slot]).start()
            return carry
        lax.fori_loop(0, tb // DMA_UNROLL, body, 0)

    def scatter(r):
        return pltpu.make_async_copy(yb.at[pl.ds(r, 1), :], y_hbm.at[pl.ds(dst_ref[0, 0, r], 1), :], sem_out)

    def first_rows(n, fn):
        def group(g, carry):
            for u in range(DMA_UNROLL):
                fn(g * DMA_UNROLL + u)
            return carry
        groups = n // DMA_UNROLL
        lax.fori_loop(0, groups, group, 0)

        def single(r, carry):
            fn(r)
            return carry
        lax.fori_loop(groups * DMA_UNROLL, n, single, 0)

    @pl.when(i < n_active)
    def _():
        @pl.when(i == 0)
        def _():
            start_gather(src_ref, 0)

        @pl.when(i + 1 < n_active)
        def _():
            start_gather(src_next_ref, 1 - slot)

        changed = jnp.logical_or(i == 0, be_ref[i] != be_ref[jnp.maximum(i - 1, 0)])

        @pl.when(changed)
        def _():
            wgb[...] = wg_ref[...].astype(BF16)
            wub[...] = wu_ref[...].astype(BF16)
            wdb[...] = wd_ref[...].astype(BF16)

        pltpu.make_async_copy(h_hbm.at[pl.ds(0, tb), :], xg.at[slot], sem_in.at[slot]).wait()
        x = xg[slot].astype(BF16)
        a = _dot(x, wgb[...])
        a = a * jax.nn.sigmoid(a) * _dot(x, wub[...])
        y = _dot(a.astype(BF16), wdb[...]) * gate_ref[...]

        @pl.when(i > 0)
        def _():
            first_rows(nval_ref[jnp.maximum(i - 1, 0)], lambda r: scatter(r).wait())

        yb[...] = y
        first_rows(nval_ref[i], lambda r: scatter(r).start())

        @pl.when(i == n_active - 1)
        def _():
            first_rows(nval_ref[i], lambda r: scatter(r).wait())


def _moe_experts(h2_flat, blk_expert, n_active, n_valid, src_tok, dst_row, gate_rows, w_gate, w_up, w_down, layer,
                 n_out_rows):
    n_rows = src_tok.shape[0]
    tb = MOE_ROWS
    n_blk = n_rows // tb
    d = h2_flat.shape[1]
    de = w_gate.shape[3]
    idx = pl.BlockSpec((1, 1, tb), lambda i, *_: (i, 0, 0), memory_space=pltpu.SMEM)
    idx_next = pl.BlockSpec((1, 1, tb), lambda i, *_: (jnp.minimum(i + 1, n_blk - 1), 0, 0), memory_space=pltpu.SMEM)
    src3 = src_tok.reshape(n_blk, 1, tb)
    return pl.pallas_call(
        _moe_kernel,
        grid_spec=pltpu.PrefetchScalarGridSpec(
            num_scalar_prefetch=3,
            grid=(n_blk,),
            in_specs=[
                idx, idx_next, idx,
                pl.BlockSpec((tb, 1), lambda i, *_: (i, 0)),
                pl.BlockSpec(memory_space=pl.ANY),
                pl.BlockSpec((None, None, d, de), lambda i, be, *_: (layer, be[i], 0, 0)),
                pl.BlockSpec((None, None, d, de), lambda i, be, *_: (layer, be[i], 0, 0)),
                pl.BlockSpec((None, None, de, d), lambda i, be, *_: (layer, be[i], 0, 0)),
            ],
            out_specs=pl.BlockSpec(memory_space=pl.ANY),
            scratch_shapes=[
                pltpu.VMEM((2, tb, d), F32), pltpu.VMEM((tb, d), F32),
                pltpu.VMEM((d, de), BF16), pltpu.VMEM((d, de), BF16), pltpu.VMEM((de, d), BF16),
                pltpu.SemaphoreType.DMA((2,)), pltpu.SemaphoreType.DMA(()),
            ],
        ),
        out_shape=jax.ShapeDtypeStruct((n_out_rows, d), F32),
        compiler_params=_params("arbitrary"),
    )(blk_expert, n_active, n_valid, src3, src3, dst_row.reshape(n_blk, 1, tb),
      gate_rows.reshape(n_rows, 1), h2_flat, w_gate, w_up, w_down)


def _route_tables(rt, n_tok):
    tb = MOE_ROWS
    m = 2 * n_tok
    e = jnp.stack([rt[:, 0, :], rt[:, 1, :]], axis=-1).reshape(m).astype(I32)
    g = jnp.stack([rt[:, 2, :], rt[:, 3, :]], axis=-1).reshape(m)
    onehot = (e[:, None] == jnp.arange(N_EXPERTS, dtype=I32)[None, :]).astype(I32)
    counts = onehot.sum(0)
    rank = jnp.sum((jnp.cumsum(onehot, axis=0) - onehot) * onehot, axis=1)
    padded = (counts + tb - 1) // tb * tb
    pad_ends = jnp.cumsum(padded)
    pad_starts = pad_ends - padded
    dest = pad_starts[e] + rank
    n_blk = -(-(m + N_EXPERTS * (tb - 1)) // tb)
    n_rows = n_blk * tb
    pair = jnp.arange(m, dtype=I32)
    packed = jnp.stack([pair // 2, (pair % 2) * n_tok + pair // 2, lax.bitcast_convert_type(g, I32)], axis=1)
    table = jnp.zeros((n_rows, 3), I32).at[dest].set(packed)
    src_tok, dst_row, gate_rows = table[:, 0], table[:, 1], lax.bitcast_convert_type(table[:, 2], F32)
    blk_start = jnp.arange(n_blk, dtype=I32) * tb
    blk_expert = jnp.minimum(jnp.sum(blk_start[:, None] >= pad_ends[None, :], axis=1), N_EXPERTS - 1).astype(I32)
    n_active = (pad_ends[-1] // tb).astype(I32).reshape(1)
    n_valid = jnp.clip(counts[blk_expert] - (blk_start - pad_starts[blk_expert]), 0, tb).astype(I32)
    return blk_expert, n_active, n_valid, src_tok, dst_row, gate_rows, m


def _final_kernel(x_ref, y0_ref, y1_ref, g2_ref, lng_ref, lnb_ref, o_ref, *, alpha):
    ffn = y0_ref[...] + y1_ref[...]
    o_ref[...] = _layer_norm(alpha * x_ref[...] + g2_ref[...] * ffn, lng_ref[...], lnb_ref[...])


def _final_ln(x, y_slots, mod, layer, ln_g, ln_b, alpha, tm):
    b, t, d = x.shape
    tok = pl.BlockSpec((None, tm, d), lambda bi, ti: (bi, ti, 0))
    vec = pl.BlockSpec((1, d), lambda bi, ti: (0, 0))
    tiles = t // tm

    def slot(k):
        return pl.BlockSpec((None, tm, d), lambda bi, ti: (k, bi * tiles + ti, 0))

    return pl.pallas_call(
        functools.partial(_final_kernel, alpha=alpha),
        grid=(b, tiles),
        in_specs=[tok, slot(0), slot(1), _mod_spec(mod, layer, 5, tm), vec, vec],
        out_specs=tok,
        out_shape=jax.ShapeDtypeStruct((b, t, d), F32),
        compiler_params=_params("arbitrary", "arbitrary"),
    )(x, y_slots, y_slots, mod, ln_g, ln_b)


def _run_trunk(x, mod, pos0, cache_k, cache_v, page_table, pool_state, prm, tm):
    b, t, d = x.shape
    depth = prm["depth"]
    alpha = (2.0 * depth) ** 0.25
    decode = cache_k is not None
    new_k, new_v, new_pool = [], [], []
    for i in range(depth):
        j = i // 2
        if i % 2 == 0:
            if decode:
                k, v, q = _qkv(x, mod, i, prm["w_qkv_b"][j], tm, prompt=False)
                per_head = (t, N_HEADS, HEAD_DIM)
                o = _moba_decode(q.astype(F32).reshape(per_head), k.reshape(per_head), v.reshape(per_head),
                                 cache_k, cache_v, j, page_table, prm["bias_decode"])
                o = o.reshape(1, t, d).astype(BF16)
            else:
                k, v, qt, vt, kh, km = _qkv(x, mod, i, prm["w_qkv_b"][j], tm, prompt=True)
                o = _moba_prompt(qt, kh, vt, km.transpose(0, 2, 1, 3), prm["bias_prompt"], prm["block_steps"])
            new_k.append(k)
            new_v.append(v)
            w_b, colscale = prm["w_o_b"][j], prm["ones"]
        else:
            if decode:
                o, hist_new = _pool_decode(x, mod, i, pool_state[j], pos0)
            else:
                o, hist_new = _pool_prompt(x, mod, i, tm)
            new_pool.append(hist_new)
            w_b, colscale = prm["w_pool_b"][j], prm["pool_scale"][j]
        x, h2, rt = _post_mixer(o, x, mod, i, w_b, colscale, prm["ln_g"][i, 0], prm["ln_b"][i, 0],
                                prm["wr_hi"], prm["wr_lo"], prm["b_router"], alpha, tm)
        n_tok = b * t
        blk_expert, n_active, n_valid, src_tok, dst_row, gate_rows, n_out_rows = _route_tables(rt, n_tok)
        y = _moe_experts(h2.reshape(n_tok, d), blk_expert, n_active, n_valid, src_tok, dst_row, gate_rows,
                         prm["w_gate"], prm["w_up"], prm["w_down"], i, n_out_rows)
        x = _final_ln(x, y.reshape(2, n_tok, d), mod, i, prm["ln_g"][i, 1], prm["ln_b"][i, 1], alpha, tm)
    return x, new_k, new_v, new_pool


def kernel(x_prompt, x_sample, cache_k, cache_v, state_pool, page_table, c_prompt, c_sample, w_ada, b_ada, ln_g, ln_b, w_qkv, w_o, w_pool, pool_scale, w_router, b_router, w_gate, w_up, w_down):
    bp, seq, d = x_prompt.shape
    db, dec_seq, _ = x_sample.shape
    depth = w_ada.shape[0]
    assert dec_seq == 1 and d == N_HEADS * HEAD_DIM and seq % MOBA_BLOCK == 0
    pos0 = page_table.shape[1] * cache_k.shape[2]

    c_all = jnp.concatenate([c_prompt, c_sample], axis=0)
    pad = (-c_all.shape[0]) % 8
    c_all = jnp.pad(c_all, ((0, pad), (0, 0)))
    mod = _adaln(c_all, w_ada, b_ada).reshape(depth, c_all.shape[0], 6, d).transpose(0, 2, 1, 3)
    mod_p = mod[:, :, :bp].reshape(depth, 6, bp, 1, d)
    mod_s = mod[:, :, bp:bp + db].reshape(depth, 6, 1, db, d)

    group = d // len(POOL_WINDOWS)
    w_pool_dense = jnp.zeros((w_pool.shape[0], d, d), F32)
    for g in range(len(POOL_WINDOWS)):
        w_pool_dense = w_pool_dense.at[:, g * group:(g + 1) * group, g * group:(g + 1) * group].set(w_pool[:, g])
    wr_hi, wr_lo = _split_bf16(w_router.T)
    slopes2 = LOG2E * 2.0 ** (-(8.0 / N_HEADS) * jnp.arange(1, N_HEADS + 1, dtype=F32))
    kq = jnp.arange(MOBA_BLOCK, dtype=I32)
    bias_prompt = -slopes2[:, None, None] * (kq[None, None, :] - kq[None, :, None]).astype(F32)
    bias_decode = -slopes2[:, None] * (pos0 - jnp.arange(pos0, dtype=I32)).astype(F32)[None, :]
    cache_kt = cache_k.transpose(0, 1, 3, 4, 2)
    cache_vt = cache_v.transpose(0, 1, 3, 4, 2)
    prm = dict(
        depth=depth,
        w_qkv_b=w_qkv.astype(BF16), w_o_b=w_o.astype(BF16), w_pool_b=w_pool_dense.astype(BF16),
        pool_scale=pool_scale.reshape(-1, 1, d), ones=jnp.ones((1, d), F32),
        ln_g=ln_g.reshape(depth, 2, 1, d), ln_b=ln_b.reshape(depth, 2, 1, d),
        wr_hi=wr_hi, wr_lo=wr_lo, b_router=b_router.reshape(N_EXPERTS, 1),
        w_gate=w_gate, w_up=w_up, w_down=w_down,
        bias_prompt=bias_prompt, block_steps=slopes2 * MOBA_BLOCK, bias_decode=bias_decode,
    )

    y_p, k_p, v_p, pool_p = _run_trunk(x_prompt, mod_p, 0, None, None, None, None, prm, tm=MOBA_BLOCK)
    y_s, k_s, v_s, pool_s = _run_trunk(
        x_sample.reshape(1, db, d), mod_s, pos0,
        cache_kt, cache_vt, page_table, state_pool, prm, tm=db)

    def heads(ts, lead):
        return jnp.stack(ts).reshape(len(ts), *lead, N_HEADS, HEAD_DIM)

    return (y_p, y_s.reshape(db, 1, d),
            heads(k_p, (bp, seq)), heads(v_p, (bp, seq)), jnp.stack(pool_p),
            heads(k_s, (db, 1)), heads(v_s, (db, 1)), jnp.stack(pool_s))
```

```python
import functools

import jax
import jax.numpy as jnp
from jax import lax
from jax.experimental import pallas as pl
from jax.experimental.pallas import tpu as pltpu

F32 = jnp.float32
BF16 = jnp.bfloat16
I32 = jnp.int32

N_HEADS = 16
HEAD_DIM = 64
MOBA_BLOCK = 256
MOBA_TOPK = 3
POOL_WINDOWS = (2, 4, 8, 16)
POOL_HIST = max(POOL_WINDOWS) - 1
POOL_HALO = 32
N_EXPERTS = 16
N_EXPERT_GROUPS = 4
EXPERTS_PER_GROUP = N_EXPERTS // N_EXPERT_GROUPS
LN_EPS = 1e-5
LANES = 128
ATT_HEADS = 4
ATT_LANES = ATT_HEADS * HEAD_DIM
VT_ROWS = HEAD_DIM + 16
MOE_ROWS = 256
DMA_UNROLL = 8
DECODE_PAGES_PER_STEP = 8
VMEM_LIMIT = 48 * 1024 * 1024
NEG_INF = float("-inf")
LOG2E = 1.4426950408889634
Q_PRESCALE = HEAD_DIM ** -0.5 * LOG2E


def _dot(a, b):
    return jnp.dot(a, b, preferred_element_type=F32)


def _dot_nt(a, b):
    return lax.dot_general(a, b, (((1,), (1,)), ((), ())), preferred_element_type=F32)


def _split_bf16(x):
    hi = x.astype(BF16)
    lo = (x - hi.astype(F32)).astype(BF16)
    return hi, lo


def _params(*sem):
    return pltpu.CompilerParams(dimension_semantics=sem, vmem_limit_bytes=VMEM_LIMIT)


def _adaln_kernel(c_ref, w_ref, b_ref, o_ref):
    c = c_ref[...]
    s_hi, s_lo = _split_bf16(c * jax.nn.sigmoid(c))
    w_hi, w_lo = _split_bf16(w_ref[...])
    o_ref[...] = _dot(s_hi, w_hi) + _dot(s_lo, w_hi) + _dot(s_hi, w_lo) + b_ref[...]


def _adaln(c, w_ada, b_ada):
    depth, d, d6 = w_ada.shape
    bc = c.shape[0]
    tn = 1024
    return pl.pallas_call(
        _adaln_kernel,
        grid=(depth, d6 // tn),
        in_specs=[
            pl.BlockSpec((bc, d), lambda l, j: (0, 0)),
            pl.BlockSpec((None, d, tn), lambda l, j: (l, 0, j)),
            pl.BlockSpec((None, 1, tn), lambda l, j: (l, 0, j)),
        ],
        out_specs=pl.BlockSpec((None, bc, tn), lambda l, j: (l, 0, j)),
        out_shape=jax.ShapeDtypeStruct((depth, bc, d6), F32),
        compiler_params=_params("arbitrary", "arbitrary"),
    )(c, w_ada, b_ada.reshape(depth, 1, d6))


def _mod_spec(mod, layer, which, tm):
    r = mod.shape[3]
    d = mod.shape[4]
    if r == 1:
        return pl.BlockSpec((None, None, None, 1, d), lambda b, t: (layer, which, b, 0, 0))
    return pl.BlockSpec((None, None, None, tm, d), lambda b, t: (layer, which, b, t, 0))


def _qkv_kernel(x_ref, sh_ref, sc_ref, w_ref, k_ref, v_ref, *extra_refs, d, n_kblk):
    h = x_ref[...] * (1.0 + sc_ref[...]) + sh_ref[...]
    hb = h.astype(BF16)
    q = _dot(hb, w_ref[:, 0:d]) * Q_PRESCALE
    k = _dot(hb, w_ref[:, d:2 * d])
    v = _dot(hb, w_ref[:, 2 * d:3 * d])
    k_ref[...] = k
    v_ref[...] = v
    if n_kblk == 0:
        extra_refs[0][...] = q.astype(BF16)
        return
    qt_ref, vt_ref, kh_ref, km_ref = extra_refs
    tm = k.shape[0]
    qt_ref[...] = q.T.astype(BF16)
    vt = v.T
    lane = lax.broadcasted_iota(I32, (tm, LANES), 1)
    pos = lax.broadcasted_iota(I32, (tm, LANES), 0) % MOBA_BLOCK
    key_pos = jnp.where((lane == HEAD_DIM) | (lane == HEAD_DIM + 1), pos, 0).astype(F32)
    ones_row = (lax.broadcasted_iota(I32, (VT_ROWS - HEAD_DIM, tm), 0) == 0).astype(BF16)
    for hd in range(N_HEADS):
        cols = slice(hd * HEAD_DIM, (hd + 1) * HEAD_DIM)
        pair = k[:, (hd // 2) * LANES:(hd // 2 + 1) * LANES]
        if hd % 2:
            pair = pltpu.roll(pair, HEAD_DIM, axis=1)
        kh_ref[hd] = jnp.where(lane < HEAD_DIM, pair, key_pos).astype(BF16)
        vt_ref[hd, 0:HEAD_DIM, :] = vt[cols, :].astype(BF16)
        vt_ref[hd, HEAD_DIM:VT_ROWS, :] = ones_row
        for i in range(n_kblk):
            km_ref[i, hd:hd + 1, :] = jnp.mean(k[i * MOBA_BLOCK:(i + 1) * MOBA_BLOCK, cols], axis=0, keepdims=True)


def _qkv(x, mod, layer, w_qkv_b, tm, prompt):
    b, t, d = x.shape
    n_kblk = tm // MOBA_BLOCK if prompt else 0
    tok = pl.BlockSpec((None, tm, d), lambda bi, ti: (bi, ti, 0))
    out_specs = [tok] * 2
    out_shape = [jax.ShapeDtypeStruct((b, t, d), F32)] * 2
    if prompt:
        out_specs += [pl.BlockSpec((None, d, tm), lambda bi, ti: (bi, 0, ti)),
                      pl.BlockSpec((None, N_HEADS, VT_ROWS, tm), lambda bi, ti: (bi, 0, 0, ti)),
                      pl.BlockSpec((None, N_HEADS, tm, LANES), lambda bi, ti: (bi, 0, ti, 0)),
                      pl.BlockSpec((None, n_kblk, N_HEADS, HEAD_DIM), lambda bi, ti: (bi, ti, 0, 0))]
        out_shape += [jax.ShapeDtypeStruct((b, d, t), BF16),
                      jax.ShapeDtypeStruct((b, N_HEADS, VT_ROWS, t), BF16),
                      jax.ShapeDtypeStruct((b, N_HEADS, t, LANES), BF16),
                      jax.ShapeDtypeStruct((b, t // MOBA_BLOCK, N_HEADS, HEAD_DIM), F32)]
    else:
        out_specs.append(tok)
        out_shape.append(jax.ShapeDtypeStruct((b, t, d), BF16))
    return pl.pallas_call(
        functools.partial(_qkv_kernel, d=d, n_kblk=n_kblk),
        grid=(b, t // tm),
        in_specs=[tok, _mod_spec(mod, layer, 0, tm), _mod_spec(mod, layer, 1, tm),
                  pl.BlockSpec((d, 3 * d), lambda bi, ti: (0, 0))],
        out_specs=out_specs,
        out_shape=out_shape,
        compiler_params=_params("arbitrary", "arbitrary"),
    )(x, mod, mod, w_qkv_b)


def _moba_kernel(offs_ref, qt_ref, k_ref, vt_ref, km_ref, slope_ref, o_ref, sel_scr, score_scr, *, n_blocks):
    hp = pl.program_id(1)
    own = pl.program_id(2)
    tq = MOBA_BLOCK
    blk = lax.broadcasted_iota(I32, (n_blocks, tq), 0)
    blk_f = blk.astype(F32)
    causal = lax.broadcasted_iota(I32, (MOBA_BLOCK, tq), 0) <= lax.broadcasted_iota(I32, (MOBA_BLOCK, tq), 1)
    own_start = pl.multiple_of(own * MOBA_BLOCK, MOBA_BLOCK)
    heads = range(ATT_HEADS)
    rows = [slice(hh * HEAD_DIM, (hh + 1) * HEAD_DIM) for hh in heads]
    block_step = [offs_ref[hp * ATT_HEADS + hh] for hh in heads]
    qt = [jnp.concatenate([qt_ref[rows[hh], :], slope_ref[hh]], axis=0) for hh in heads]

    state = []
    for hh in heads:
        gate = jnp.where(blk < own, _dot(km_ref[hh].astype(BF16), qt_ref[rows[hh], :]), NEG_INF)
        sel = jnp.zeros((n_blocks, tq), F32)
        for _ in range(MOBA_TOPK):
            m = jnp.max(gate, axis=0, keepdims=True)
            first = jnp.min(jnp.where(gate == m, blk_f, float(n_blocks)), axis=0, keepdims=True)
            pick = (blk_f == first) & (m > NEG_INF)
            sel = jnp.where(pick, 1.0, sel)
            gate = jnp.where(pick, NEG_INF, gate)
        sel_scr[hh] = sel

        t = jnp.where(causal, _dot(k_ref[hh, pl.ds(own_start, MOBA_BLOCK), :], qt[hh]), NEG_INF)
        m_i = jnp.max(t, axis=0, keepdims=True)
        p = jnp.exp2((t - m_i).astype(BF16))
        acc = _dot(vt_ref[hh, :, pl.ds(own_start, MOBA_BLOCK)], p)
        state.append((m_i, acc))

    def scores_into(half, n):
        start = pl.multiple_of(jnp.minimum(n, n_blocks - 1) * MOBA_BLOCK, MOBA_BLOCK)
        for hh in heads:
            score_scr[half, hh] = _dot(k_ref[hh, pl.ds(start, MOBA_BLOCK), :], qt[hh])

    def consume(half, n, state):
        start = pl.multiple_of(jnp.minimum(n, n_blocks - 1) * MOBA_BLOCK, MOBA_BLOCK)
        blocks_back = (own - n).astype(F32)
        new_state = []
        for hh in heads:
            m_i, acc = state[hh]
            off = blocks_back * block_step[hh]
            t = score_scr[half, hh]
            chosen = sel_scr[hh, pl.ds(jnp.minimum(n, n_blocks - 1), 1), :] > 0.0
            m_new = jnp.maximum(m_i, jnp.where(chosen, jnp.max(t, axis=0, keepdims=True) - off, NEG_INF))
            alpha = jnp.exp2(m_i - m_new)
            p = jnp.exp2((t - jnp.where(chosen, m_new + off, float("inf"))).astype(BF16))
            acc_new = alpha * acc + _dot(vt_ref[hh, :, pl.ds(start, MOBA_BLOCK)], p)
            new_state.append((m_new, acc_new))
        return tuple(new_state)

    def body(j, state):
        n = 2 * j
        scores_into(1, n + 1)
        state = consume(0, n, state)
        scores_into(0, n + 2)
        return consume(1, n + 1, state)

    scores_into(0, 0)
    state = lax.fori_loop(0, (own + 1) // 2, body, tuple(state))
    out = jnp.concatenate([acc[0:HEAD_DIM] / acc[HEAD_DIM:HEAD_DIM + 1] for _, acc in state], axis=0)
    o_ref[...] = out.T.astype(BF16)


def _moba_prompt(qt, kh, vt, kmean, slope_rows, block_steps):
    b, d, t = qt.shape
    n_blocks = t // MOBA_BLOCK
    return pl.pallas_call(
        functools.partial(_moba_kernel, n_blocks=n_blocks),
        grid=(b, d // ATT_LANES, n_blocks),
        in_specs=[
            pl.BlockSpec(memory_space=pltpu.SMEM),
            pl.BlockSpec((None, ATT_LANES, MOBA_BLOCK), lambda bi, hp, qi: (bi, hp, qi)),
            pl.BlockSpec((None, ATT_HEADS, t, LANES), lambda bi, hp, qi: (bi, hp, 0, 0)),
            pl.BlockSpec((None, ATT_HEADS, VT_ROWS, t), lambda bi, hp, qi: (bi, hp, 0, 0)),
            pl.BlockSpec((None, ATT_HEADS, n_blocks, HEAD_DIM), lambda bi, hp, qi: (bi, hp, 0, 0)),
            pl.BlockSpec((ATT_HEADS, LANES - HEAD_DIM, MOBA_BLOCK), lambda bi, hp, qi: (hp, 0, 0)),
        ],
        out_specs=pl.BlockSpec((None, MOBA_BLOCK, ATT_LANES), lambda bi, hp, qi: (bi, qi, hp)),
        out_shape=jax.ShapeDtypeStruct((b, t, d), BF16),
        scratch_shapes=[pltpu.VMEM((ATT_HEADS, n_blocks, MOBA_BLOCK), F32),
                        pltpu.VMEM((2, ATT_HEADS, MOBA_BLOCK, MOBA_BLOCK), F32)],
        compiler_params=_params("arbitrary", "arbitrary", "arbitrary"),
    )(block_steps, qt, kh, vt, kmean, slope_rows)


def _dec_attn_kernel(pt_ref, qt_ref, q_ref, kn_ref, vn_ref, bias_ref, *refs, n_pages, page, pps):
    del pt_ref
    kp_refs, vp_refs = refs[:pps], refs[pps:2 * pps]
    o_ref, qb_scr, s_scr, p_scr, acc_scr, pown_scr, l_scr = refs[2 * pps:]
    s_id = pl.program_id(1)
    k_steps = n_pages // pps
    n_past_blocks = n_pages * page // MOBA_BLOCK
    q = q_ref[...]

    def lanes(n):
        return slice(n * MOBA_BLOCK, (n + 1) * MOBA_BLOCK)

    @pl.when(s_id == 0)
    def _():
        qt = qt_ref[...]
        for hd in range(N_HEADS):
            qb_scr[hd] = jnp.broadcast_to(qt[:, hd:hd + 1], (HEAD_DIM, page))

    @pl.when(s_id < k_steps)
    def _():
        for u, kp_ref in enumerate(kp_refs):
            rows = [jnp.sum(kp_ref[hd] * qb_scr[hd], axis=0, keepdims=True) for hd in range(N_HEADS)]
            start = pl.multiple_of((s_id * pps + u) * page, page)
            s_scr[:, pl.ds(start, page)] = jnp.concatenate(rows, axis=0)

    @pl.when(s_id == k_steps - 1)
    def _():
        gates = [jnp.sum(s_scr[:, lanes(n)], axis=1, keepdims=True) / MOBA_BLOCK for n in range(n_past_blocks)]
        sel = [jnp.zeros((N_HEADS, 1), F32) for _ in gates]
        for _ in range(MOBA_TOPK):
            m = functools.reduce(jnp.maximum, gates)
            first = jnp.full((N_HEADS, 1), n_past_blocks, I32)
            for n in reversed(range(n_past_blocks)):
                first = jnp.where(gates[n] == m, n, first)
            for n in range(n_past_blocks):
                pick = (first == n) & (m > NEG_INF)
                sel[n] = jnp.where(pick, 1.0, sel[n])
                gates[n] = jnp.where(pick, NEG_INF, gates[n])

        def logits(n):
            return jnp.where(sel[n] > 0.0, s_scr[:, lanes(n)] + bias_ref[:, lanes(n)], NEG_INF)

        s_own = jnp.sum(q * kn_ref[...], axis=1, keepdims=True)
        m = functools.reduce(jnp.maximum, [jnp.max(logits(n), axis=1, keepdims=True)
                                           for n in range(n_past_blocks)] + [s_own])
        l = jnp.exp2(s_own - m)
        pown_scr[...] = l
        for n in range(n_past_blocks):
            p = jnp.exp2(logits(n) - m)
            l = l + jnp.sum(p, axis=1, keepdims=True)
            p_scr[:, lanes(n)] = p
        l_scr[...] = l
        acc_scr[...] = jnp.zeros_like(acc_scr)

    @pl.when(s_id >= k_steps)
    def _():
        for hd in range(N_HEADS):
            part = acc_scr[hd]
            for u, vp_ref in enumerate(vp_refs):
                start = pl.multiple_of(((s_id - k_steps) * pps + u) * page, page)
                part = part + vp_ref[hd] * p_scr[hd:hd + 1, pl.ds(start, page)]
            acc_scr[hd] = part

    @pl.when(s_id == 2 * k_steps - 1)
    def _():
        ones = jnp.ones((8, page), BF16)
        rows = []
        for hd in range(N_HEADS):
            hi, lo = _split_bf16(acc_scr[hd])
            rows.append((_dot_nt(ones, hi) + _dot_nt(ones, lo))[0:1, :])
        past_part = jnp.concatenate(rows, axis=0)
        o_ref[...] = (past_part + pown_scr[...] * vn_ref[...]) / l_scr[...]


def _moba_decode(q, k_new, v_new, cache_k, cache_v, att_layer, page_table, bias):
    db = q.shape[0]
    n_pages = page_table.shape[1]
    page = cache_k.shape[4]
    past = n_pages * page
    assert past % MOBA_BLOCK == 0 and MOBA_BLOCK % page == 0 and page % LANES == 0
    row = pl.BlockSpec((None, N_HEADS, HEAD_DIM), lambda b, s, pt: (b, 0, 0))
    page_block = (None, None, N_HEADS, HEAD_DIM, page)

    pps = max(u for u in range(1, DECODE_PAGES_PER_STEP + 1) if n_pages % u == 0)
    k_steps = n_pages // pps

    def k_spec(u):
        return pl.BlockSpec(page_block, lambda b, s, pt: (
            att_layer, pt[b * n_pages + jnp.minimum(s, k_steps - 1) * pps + u], 0, 0, 0))

    def v_spec(u):
        return pl.BlockSpec(page_block, lambda b, s, pt: (
            att_layer, pt[b * n_pages + jnp.maximum(s - k_steps, 0) * pps + u], 0, 0, 0))

    return pl.pallas_call(
        functools.partial(_dec_attn_kernel, n_pages=n_pages, page=page, pps=pps),
        grid_spec=pltpu.PrefetchScalarGridSpec(
            num_scalar_prefetch=1,
            grid=(db, 2 * k_steps),
            in_specs=[
                pl.BlockSpec((None, HEAD_DIM, N_HEADS), lambda b, s, pt: (b, 0, 0)),
                row, row, row,
                pl.BlockSpec((N_HEADS, past), lambda b, s, pt: (0, 0)),
                *[k_spec(u) for u in range(pps)],
                *[v_spec(u) for u in range(pps)],
            ],
            out_specs=row,
            scratch_shapes=[
                pltpu.VMEM((N_HEADS, HEAD_DIM, page), F32),
                pltpu.VMEM((N_HEADS, past), F32),
                pltpu.VMEM((N_HEADS, past), F32),
                pltpu.VMEM((N_HEADS, HEAD_DIM, page), F32),
                pltpu.VMEM((N_HEADS, 1), F32),
                pltpu.VMEM((N_HEADS, 1), F32),
            ],
        ),
        out_shape=jax.ShapeDtypeStruct((db, N_HEADS, HEAD_DIM), F32),
        compiler_params=_params("arbitrary", "arbitrary"),
    )(page_table.reshape(-1), q.transpose(0, 2, 1), q, k_new, v_new, bias, *([cache_k] * pps), *([cache_v] * pps))


def _pool_counts(pos, width):
    return jnp.minimum(pos + 1, width).astype(F32)


def _pool_prompt_kernel(x_ref, sh_ref, sc_ref, o_ref, hist_ref, ext, buf_a, buf_b, *, tm, group):
    t = pl.program_id(1)
    rows = POOL_HALO + tm

    @pl.when(t == 0)
    def _():
        ext[0:POOL_HALO, :] = jnp.zeros((POOL_HALO, ext.shape[1]), F32)

    h = x_ref[...] * (1.0 + sc_ref[...]) + sh_ref[...]
    ext[POOL_HALO:rows, :] = h
    d = h.shape[1]
    buf_a[8:rows, :] = ext[8:rows, :] + ext[7:rows - 1, :]
    buf_b[16:rows, group:d] = buf_a[16:rows, group:d] + buf_a[14:rows - 2, group:d]
    buf_a[24:rows, 2 * group:d] = buf_b[24:rows, 2 * group:d] + buf_b[20:rows - 4, 2 * group:d]
    buf_b[32:rows, 3 * group:d] = buf_a[32:rows, 3 * group:d] + buf_a[24:rows - 8, 3 * group:d]
    pos = t * tm + lax.broadcasted_iota(I32, (tm, 1), 0)
    sums = (buf_a, buf_b, buf_a, buf_b)
    for g, width in enumerate(POOL_WINDOWS):
        cols = slice(g * group, (g + 1) * group)
        mean = sums[g][POOL_HALO:rows, cols] / _pool_counts(pos, width)
        o_ref[:, cols] = (mean - h[:, cols]).astype(BF16)
    ext[0:POOL_HALO, :] = h[tm - POOL_HALO:tm, :]

    @pl.when(t == pl.num_programs(1) - 1)
    def _():
        hist_ref[...] = h[tm - POOL_HIST:tm, :]


def _pool_prompt(x, mod, layer, tm):
    b, t, d = x.shape
    tok = pl.BlockSpec((None, tm, d), lambda bi, ti: (bi, ti, 0))
    return pl.pallas_call(
        functools.partial(_pool_prompt_kernel, tm=tm, group=d // len(POOL_WINDOWS)),
        grid=(b, t // tm),
        in_specs=[tok, _mod_spec(mod, layer, 0, tm), _mod_spec(mod, layer, 1, tm)],
        out_specs=[tok, pl.BlockSpec((None, POOL_HIST, d), lambda bi, ti: (bi, 0, 0))],
        out_shape=[jax.ShapeDtypeStruct((b, t, d), BF16), jax.ShapeDtypeStruct((b, POOL_HIST, d), F32)],
        scratch_shapes=[pltpu.VMEM((POOL_HALO + tm, d), F32)] * 3,
        compiler_params=_params("arbitrary", "arbitrary"),
    )(x, mod, mod)


def _pool_decode_kernel(x_ref, sh_ref, sc_ref, hist_ref, o_ref, hist_out_ref, *, group, pos0):
    h = x_ref[...] * (1.0 + sc_ref[...]) + sh_ref[...]
    rows = []
    for b in range(h.shape[0]):
        hb = h[b:b + 1, :]
        parts = []
        for g, width in enumerate(POOL_WINDOWS):
            cols = slice(g * group, (g + 1) * group)
            tail = hist_ref[b, POOL_HIST - (width - 1):POOL_HIST, cols]
            total = jnp.sum(tail, axis=0, keepdims=True) + hb[:, cols]
            parts.append(total / float(min(pos0 + 1, width)) - hb[:, cols])
        rows.append(jnp.concatenate(parts, axis=1))
        hist_out_ref[b, 0:POOL_HIST - 1, :] = hist_ref[b, 1:POOL_HIST, :]
        hist_out_ref[b, POOL_HIST - 1:POOL_HIST, :] = hb
    o_ref[...] = jnp.concatenate(rows, axis=0).astype(BF16)


def _pool_decode(x, mod, layer, hist, pos0):
    _, db, d = x.shape
    bb = 16
    assert db % bb == 0 and pos0 >= POOL_HIST
    tok = pl.BlockSpec((None, bb, d), lambda bi, ti: (0, ti, 0))
    hspec = pl.BlockSpec((bb, POOL_HIST, d), lambda bi, ti: (ti, 0, 0))
    mixed, hist_new = pl.pallas_call(
        functools.partial(_pool_decode_kernel, group=d // len(POOL_WINDOWS), pos0=pos0),
        grid=(1, db // bb),
        in_specs=[tok, _mod_spec(mod, layer, 0, bb), _mod_spec(mod, layer, 1, bb), hspec],
        out_specs=[tok, hspec],
        out_shape=[jax.ShapeDtypeStruct((1, db, d), BF16), jax.ShapeDtypeStruct((db, POOL_HIST, d), F32)],
        compiler_params=_params("arbitrary", "arbitrary"),
    )(x, mod, mod, hist)
    return mixed, hist_new


def _layer_norm(z, g, b):
    mu = jnp.mean(z, axis=-1, keepdims=True)
    zc = z - mu
    var = jnp.mean(zc * zc, axis=-1, keepdims=True)
    return zc * lax.rsqrt(var + LN_EPS) * g + b


def _first_max(vals):
    m = functools.reduce(jnp.maximum, vals)
    idx = jnp.full(m.shape, len(vals) - 1, I32)
    for i in reversed(range(len(vals) - 1)):
        idx = jnp.where(vals[i] == m, i, idx)
    return m, idx


def _top2(vals):
    m0, i0 = _first_max(vals)
    rest = [jnp.where(i0 == i, NEG_INF, v) for i, v in enumerate(vals)]
    m1, i1 = _first_max(rest)
    return m0, i0, m1, i1


def _post_kernel(o_ref, x_ref, g1_ref, sh2_ref, sc2_ref, w_ref, cs_ref, lng_ref, lnb_ref,
                 wr_hi_ref, wr_lo_ref, br_ref, xo_ref, h2_ref, rt_ref, *, alpha):
    out = _dot(o_ref[...], w_ref[...]) * cs_ref[...]
    xn = _layer_norm(alpha * x_ref[...] + g1_ref[...] * out, lng_ref[...], lnb_ref[...])
    xo_ref[...] = xn
    h2 = xn * (1.0 + sc2_ref[...]) + sh2_ref[...]
    h2_ref[...] = h2
    hi, lo = _split_bf16(h2)
    logits = _dot_nt(wr_hi_ref[...], hi) + _dot_nt(wr_hi_ref[...], lo) + _dot_nt(wr_lo_ref[...], hi) + br_ref[...]
    aff = jax.nn.sigmoid(logits)
    rows = [aff[e:e + 1, :] for e in range(N_EXPERTS)]
    gscores = []
    for g in range(N_EXPERT_GROUPS):
        m0, _, m1, _ = _top2(rows[g * EXPERTS_PER_GROUP:(g + 1) * EXPERTS_PER_GROUP])
        gscores.append(m0 + m1)
    _, g_sel = _first_max(gscores)
    in_grp = []
    for i in range(EXPERTS_PER_GROUP):
        v = rows[i]
        for g in range(1, N_EXPERT_GROUPS):
            v = jnp.where(g_sel == g, rows[g * EXPERTS_PER_GROUP + i], v)
        in_grp.append(v)
    w0, i0, w1, i1 = _top2(in_grp)
    denom = w0 + w1
    e0 = (g_sel * EXPERTS_PER_GROUP + i0).astype(F32)
    e1 = (g_sel * EXPERTS_PER_GROUP + i1).astype(F32)
    zero = jnp.zeros_like(w0)
    rt_ref[...] = jnp.concatenate([e0, e1, w0 / denom, w1 / denom, zero, zero, zero, zero], axis=0)


def _post_mixer(o, x, mod, layer, w_b, colscale, ln_g, ln_b, wr_hi, wr_lo, b_router, alpha, tm):
    b, t, d = x.shape
    tok = pl.BlockSpec((None, tm, d), lambda bi, ti: (bi, ti, 0))
    vec = pl.BlockSpec((1, d), lambda bi, ti: (0, 0))
    rtr = pl.BlockSpec((N_EXPERTS, d), lambda bi, ti: (0, 0))
    return pl.pallas_call(
        functools.partial(_post_kernel, alpha=alpha),
        grid=(b, t // tm),
        in_specs=[tok, tok, _mod_spec(mod, layer, 2, tm), _mod_spec(mod, layer, 3, tm), _mod_spec(mod, layer, 4, tm),
                  pl.BlockSpec((d, d), lambda bi, ti: (0, 0)), vec, vec, vec, rtr, rtr,
                  pl.BlockSpec((N_EXPERTS, 1), lambda bi, ti: (0, 0))],
        out_specs=[tok, tok, pl.BlockSpec((None, 8, tm), lambda bi, ti: (bi, 0, ti))],
        out_shape=[jax.ShapeDtypeStruct((b, t, d), F32), jax.ShapeDtypeStruct((b, t, d), F32),
                   jax.ShapeDtypeStruct((b, 8, t), F32)],
        compiler_params=_params("arbitrary", "arbitrary"),
    )(o, x, mod, mod, mod, w_b, colscale, ln_g, ln_b, wr_hi, wr_lo, b_router)


def _moe_kernel(be_ref, nact_ref, nval_ref, src_ref, src_next_ref, dst_ref, gate_ref, h_hbm, wg_ref, wu_ref, wd_ref,
                y_hbm, xg, yb, wgb, wub, wdb, sem_in, sem_out):
    i = pl.program_id(0)
    tb = xg.shape[1]
    n_active = nact_ref[0]
    slot = lax.rem(i, 2)

    def start_gather(idx_ref, to_slot):
        def body(g, carry):
            for u in range(DMA_UNROLL):
                r = g * DMA_UNROLL + u
                pltpu.make_async_copy(h_hbm.at[pl.ds(idx_ref[0, 0, r], 1), :], xg.at[to_slot, pl.ds(r, 1), :],
                                      sem_in.at[to_slot]).start()
            return carry
        lax.fori_loop(0, tb // DMA_UNROLL, body, 0)

    def scatter(r):
        return pltpu.make_async_copy(yb.at[pl.ds(r, 1), :], y_hbm.at[pl.ds(dst_ref[0, 0, r], 1), :], sem_out)

    def first_rows(n, fn):
        def group(g, carry):
            for u in range(DMA_UNROLL):
                fn(g * DMA_UNROLL + u)
            return carry
        groups = n // DMA_UNROLL
        lax.fori_loop(0, groups, group, 0)

        def single(r, carry):
            fn(r)
            return carry
        lax.fori_loop(groups * DMA_UNROLL, n, single, 0)

    @pl.when(i < n_active)
    def _():
        @pl.when(i == 0)
        def _():
            start_gather(src_ref, 0)

        changed = jnp.logical_or(i == 0, be_ref[i] != be_ref[jnp.maximum(i - 1, 0)])

        @pl.when(changed)
        def _():
            wgb[...] = wg_ref[...].astype(BF16)
            wub[...] = wu_ref[...].astype(BF16)
            wdb[...] = wd_ref[...].astype(BF16)

        pltpu.make_async_copy(h_hbm.at[pl.ds(0, tb), :], xg.at[slot], sem_in.at[slot]).wait()

        @pl.when(i > 0)
        def _():
            first_rows(nval_ref[jnp.maximum(i - 1, 0)], lambda r: scatter(r).wait())

        def expert(cur):
            x = xg[cur].astype(BF16)
            a = _dot(x, wgb[...])
            a = a * jax.nn.sigmoid(a) * _dot(x, wub[...])
            yb[...] = _dot(a.astype(BF16), wdb[...]) * gate_ref[...]

        for cur in range(2):
            @pl.when(jnp.logical_and(slot == cur, i + 1 < n_active))
            def _(cur=cur):
                for r in range(tb):
                    pltpu.make_async_copy(h_hbm.at[pl.ds(src_next_ref[0, 0, r], 1), :],
                                          xg.at[1 - cur, pl.ds(r, 1), :], sem_in.at[1 - cur]).start()
                expert(cur)

            @pl.when(jnp.logical_and(slot == cur, i + 1 >= n_active))
            def _(cur=cur):
                expert(cur)

        first_rows(nval_ref[i], lambda r: scatter(r).start())

        @pl.when(i == n_active - 1)
        def _():
            first_rows(nval_ref[i], lambda r: scatter(r).wait())


def _moe_experts(h2_flat, blk_expert, n_active, n_valid, src_tok, dst_row, gate_rows, w_gate, w_up, w_down, layer,
                 n_out_rows):
    n_rows = src_tok.shape[0]
    tb = MOE_ROWS
    n_blk = n_rows // tb
    d = h2_flat.shape[1]
    de = w_gate.shape[3]
    idx = pl.BlockSpec((1, 1, tb), lambda i, *_: (i, 0, 0), memory_space=pltpu.SMEM)
    idx_next = pl.BlockSpec((1, 1, tb), lambda i, *_: (jnp.minimum(i + 1, n_blk - 1), 0, 0), memory_space=pltpu.SMEM)
    src3 = src_tok.reshape(n_blk, 1, tb)
    return pl.pallas_call(
        _moe_kernel,
        grid_spec=pltpu.PrefetchScalarGridSpec(
            num_scalar_prefetch=3,
            grid=(n_blk,),
            in_specs=[
                idx, idx_next, idx,
                pl.BlockSpec((tb, 1), lambda i, *_: (i, 0)),
                pl.BlockSpec(memory_space=pl.ANY),
                pl.BlockSpec((None, None, d, de), lambda i, be, *_: (layer, be[i], 0, 0)),
                pl.BlockSpec((None, None, d, de), lambda i, be, *_: (layer, be[i], 0, 0)),
                pl.BlockSpec((None, None, de, d), lambda i, be, *_: (layer, be[i], 0, 0)),
            ],
            out_specs=pl.BlockSpec(memory_space=pl.ANY),
            scratch_shapes=[
                pltpu.VMEM((2, tb, d), F32), pltpu.VMEM((tb, d), F32),
                pltpu.VMEM((d, de), BF16), pltpu.VMEM((d, de), BF16), pltpu.VMEM((de, d), BF16),
                pltpu.SemaphoreType.DMA((2,)), pltpu.SemaphoreType.DMA(()),
            ],
        ),
        out_shape=jax.ShapeDtypeStruct((n_out_rows, d), F32),
        compiler_params=_params("arbitrary"),
    )(blk_expert, n_active, n_valid, src3, src3, dst_row.reshape(n_blk, 1, tb),
      gate_rows.reshape(n_rows, 1), h2_flat, w_gate, w_up, w_down)


def _route_tables(rt, n_tok):
    tb = MOE_ROWS
    m = 2 * n_tok
    e = jnp.stack([rt[:, 0, :], rt[:, 1, :]], axis=-1).reshape(m).astype(I32)
    g = jnp.stack([rt[:, 2, :], rt[:, 3, :]], axis=-1).reshape(m)
    onehot = (e[:, None] == jnp.arange(N_EXPERTS, dtype=I32)[None, :]).astype(I32)
    counts = onehot.sum(0)
    rank = jnp.sum((jnp.cumsum(onehot, axis=0) - onehot) * onehot, axis=1)
    padded = (counts + tb - 1) // tb * tb
    pad_ends = jnp.cumsum(padded)
    pad_starts = pad_ends - padded
    dest = pad_starts[e] + rank
    n_blk = -(-(m + N_EXPERTS * (tb - 1)) // tb)
    n_rows = n_blk * tb
    pair = jnp.arange(m, dtype=I32)
    packed = jnp.stack([pair // 2, (pair % 2) * n_tok + pair // 2, lax.bitcast_convert_type(g, I32)], axis=1)
    table = jnp.zeros((n_rows, 3), I32).at[dest].set(packed)
    src_tok, dst_row, gate_rows = table[:, 0], table[:, 1], lax.bitcast_convert_type(table[:, 2], F32)
    blk_start = jnp.arange(n_blk, dtype=I32) * tb
    blk_expert = jnp.minimum(jnp.sum(blk_start[:, None] >= pad_ends[None, :], axis=1), N_EXPERTS - 1).astype(I32)
    n_active = (pad_ends[-1] // tb).astype(I32).reshape(1)
    n_valid = jnp.clip(counts[blk_expert] - (blk_start - pad_starts[blk_expert]), 0, tb).astype(I32)
    return blk_expert, n_active, n_valid, src_tok, dst_row, gate_rows, m


def _final_kernel(x_ref, y0_ref, y1_ref, g2_ref, lng_ref, lnb_ref, o_ref, *, alpha):
    ffn = y0_ref[...] + y1_ref[...]
    o_ref[...] = _layer_norm(alpha * x_ref[...] + g2_ref[...] * ffn, lng_ref[...], lnb_ref[...])


def _final_ln(x, y_slots, mod, layer, ln_g, ln_b, alpha, tm):
    b, t, d = x.shape
    tok = pl.BlockSpec((None, tm, d), lambda bi, ti: (bi, ti, 0))
    vec = pl.BlockSpec((1, d), lambda bi, ti: (0, 0))
    tiles = t // tm

    def slot(k):
        return pl.BlockSpec((None, tm, d), lambda bi, ti: (k, bi * tiles + ti, 0))

    return pl.pallas_call(
        functools.partial(_final_kernel, alpha=alpha),
        grid=(b, tiles),
        in_specs=[tok, slot(0), slot(1), _mod_spec(mod, layer, 5, tm), vec, vec],
        out_specs=tok,
        out_shape=jax.ShapeDtypeStruct((b, t, d), F32),
        compiler_params=_params("arbitrary", "arbitrary"),
    )(x, y_slots, y_slots, mod, ln_g, ln_b)


def _run_trunk(x, mod, pos0, cache_k, cache_v, page_table, pool_state, prm, tm):
    b, t, d = x.shape
    depth = prm["depth"]
    alpha = (2.0 * depth) ** 0.25
    decode = cache_k is not None
    new_k, new_v, new_pool = [], [], []
    for i in range(depth):
        j = i // 2
        if i % 2 == 0:
            if decode:
                k, v, q = _qkv(x, mod, i, prm["w_qkv_b"][j], tm, prompt=False)
                per_head = (t, N_HEADS, HEAD_DIM)
                o = _moba_decode(q.astype(F32).reshape(per_head), k.reshape(per_head), v.reshape(per_head),
                                 cache_k, cache_v, j, page_table, prm["bias_decode"])
                o = o.reshape(1, t, d).astype(BF16)
            else:
                k, v, qt, vt, kh, km = _qkv(x, mod, i, prm["w_qkv_b"][j], tm, prompt=True)
                o = _moba_prompt(qt, kh, vt, km.transpose(0, 2, 1, 3), prm["slope_rows"], prm["block_steps"])
            new_k.append(k)
            new_v.append(v)
            w_b, colscale = prm["w_o_b"][j], prm["ones"]
        else:
            if decode:
                o, hist_new = _pool_decode(x, mod, i, pool_state[j], pos0)
            else:
                o, hist_new = _pool_prompt(x, mod, i, tm)
            new_pool.append(hist_new)
            w_b, colscale = prm["w_pool_b"][j], prm["pool_scale"][j]
        x, h2, rt = _post_mixer(o, x, mod, i, w_b, colscale, prm["ln_g"][i, 0], prm["ln_b"][i, 0],
                                prm["wr_hi"], prm["wr_lo"], prm["b_router"], alpha, tm)
        n_tok = b * t
        blk_expert, n_active, n_valid, src_tok, dst_row, gate_rows, n_out_rows = _route_tables(rt, n_tok)
        y = _moe_experts(h2.reshape(n_tok, d), blk_expert, n_active, n_valid, src_tok, dst_row, gate_rows,
                         prm["w_gate"], prm["w_up"], prm["w_down"], i, n_out_rows)
        x = _final_ln(x, y.reshape(2, n_tok, d), mod, i, prm["ln_g"][i, 1], prm["ln_b"][i, 1], alpha, tm)
    return x, new_k, new_v, new_pool


def kernel(x_prompt, x_sample, cache_k, cache_v, state_pool, page_table, c_prompt, c_sample, w_ada, b_ada, ln_g, ln_b, w_qkv, w_o, w_pool, pool_scale, w_router, b_router, w_gate, w_up, w_down):
    bp, seq, d = x_prompt.shape
    db, dec_seq, _ = x_sample.shape
    depth = w_ada.shape[0]
    assert dec_seq == 1 and d == N_HEADS * HEAD_DIM and seq % MOBA_BLOCK == 0
    pos0 = page_table.shape[1] * cache_k.shape[2]

    c_all = jnp.concatenate([c_prompt, c_sample], axis=0)
    pad = (-c_all.shape[0]) % 8
    c_all = jnp.pad(c_all, ((0, pad), (0, 0)))
    mod = _adaln(c_all, w_ada, b_ada).reshape(depth, c_all.shape[0], 6, d).transpose(0, 2, 1, 3)
    mod_p = mod[:, :, :bp].reshape(depth, 6, bp, 1, d)
    mod_s = mod[:, :, bp:bp + db].reshape(depth, 6, 1, db, d)

    group = d // len(POOL_WINDOWS)
    w_pool_dense = jnp.zeros((w_pool.shape[0], d, d), F32)
    for g in range(len(POOL_WINDOWS)):
        w_pool_dense = w_pool_dense.at[:, g * group:(g + 1) * group, g * group:(g + 1) * group].set(w_pool[:, g])
    wr_hi, wr_lo = _split_bf16(w_router.T)
    slopes2 = LOG2E * 2.0 ** (-(8.0 / N_HEADS) * jnp.arange(1, N_HEADS + 1, dtype=F32))
    c_hi, c_lo = _split_bf16(slopes2)
    slope_rows = jnp.zeros((N_HEADS, LANES - HEAD_DIM, MOBA_BLOCK), BF16)
    slope_rows = slope_rows.at[:, 0, :].set(c_hi[:, None]).at[:, 1, :].set(c_lo[:, None])
    bias_decode = -slopes2[:, None] * (pos0 - jnp.arange(pos0, dtype=I32)).astype(F32)[None, :]
    cache_kt = cache_k.transpose(0, 1, 3, 4, 2)
    cache_vt = cache_v.transpose(0, 1, 3, 4, 2)
    prm = dict(
        depth=depth,
        w_qkv_b=w_qkv.astype(BF16), w_o_b=w_o.astype(BF16), w_pool_b=w_pool_dense.astype(BF16),
        pool_scale=pool_scale.reshape(-1, 1, d), ones=jnp.ones((1, d), F32),
        ln_g=ln_g.reshape(depth, 2, 1, d), ln_b=ln_b.reshape(depth, 2, 1, d),
        wr_hi=wr_hi, wr_lo=wr_lo, b_router=b_router.reshape(N_EXPERTS, 1),
        w_gate=w_gate, w_up=w_up, w_down=w_down,
        slope_rows=slope_rows, block_steps=slopes2 * MOBA_BLOCK, bias_decode=bias_decode,
    )

    y_p, k_p, v_p, pool_p = _run_trunk(x_prompt, mod_p, 0, None, None, None, None, prm, tm=MOBA_BLOCK)
    y_s, k_s, v_s, pool_s = _run_trunk(
        x_sample.reshape(1, db, d), mod_s, pos0,
        cache_kt, cache_vt, page_table, state_pool, prm, tm=db)

    def heads(ts, lead):
        return jnp.stack(ts).reshape(len(ts), *lead, N_HEADS, HEAD_DIM)

    return (y_p, y_s.reshape(db, 1, d),
            heads(k_p, (bp, seq)), heads(v_p, (bp, seq)), jnp.stack(pool_p),
            heads(k_s, (db, 1)), heads(v_s, (db, 1)), jnp.stack(pool_s))
```

```python
import functools

import jax
import jax.numpy as jnp
from jax import lax
from jax.experimental import pallas as pl
from jax.experimental.pallas import tpu as pltpu

F32 = jnp.float32
BF16 = jnp.bfloat16
I32 = jnp.int32

N_HEADS = 16
HEAD_DIM = 64
MOBA_BLOCK = 256
MOBA_TOPK = 3
POOL_WINDOWS = (2, 4, 8, 16)
POOL_HIST = max(POOL_WINDOWS) - 1
POOL_HALO = 32
N_EXPERTS = 16
N_EXPERT_GROUPS = 4
EXPERTS_PER_GROUP = N_EXPERTS // N_EXPERT_GROUPS
LN_EPS = 1e-5
LANES = 128
ATT_HEADS = 4
ATT_LANES = ATT_HEADS * HEAD_DIM
VT_ROWS = HEAD_DIM + 16
MOE_ROWS = 256
DMA_UNROLL = 8
DECODE_PAGES_PER_STEP = 16
VMEM_LIMIT = 48 * 1024 * 1024
NEG_INF = float("-inf")
LOG2E = 1.4426950408889634
Q_PRESCALE = HEAD_DIM ** -0.5 * LOG2E


def _dot(a, b):
    return jnp.dot(a, b, preferred_element_type=F32)


def _dot_nt(a, b):
    return lax.dot_general(a, b, (((1,), (1,)), ((), ())), preferred_element_type=F32)


def _split_bf16(x):
    hi = x.astype(BF16)
    lo = (x - hi.astype(F32)).astype(BF16)
    return hi, lo


def _params(*sem):
    return pltpu.CompilerParams(dimension_semantics=sem, vmem_limit_bytes=VMEM_LIMIT)


def _adaln_kernel(c_ref, w_ref, b_ref, o_ref):
    c = c_ref[...]
    s_hi, s_lo = _split_bf16(c * jax.nn.sigmoid(c))
    w_hi, w_lo = _split_bf16(w_ref[...])
    o_ref[...] = _dot(s_hi, w_hi) + _dot(s_lo, w_hi) + _dot(s_hi, w_lo) + b_ref[...]


def _adaln(c, w_ada, b_ada):
    depth, d, d6 = w_ada.shape
    bc = c.shape[0]
    tn = 1024
    return pl.pallas_call(
        _adaln_kernel,
        grid=(depth, d6 // tn),
        in_specs=[
            pl.BlockSpec((bc, d), lambda l, j: (0, 0)),
            pl.BlockSpec((None, d, tn), lambda l, j: (l, 0, j)),
            pl.BlockSpec((None, 1, tn), lambda l, j: (l, 0, j)),
        ],
        out_specs=pl.BlockSpec((None, bc, tn), lambda l, j: (l, 0, j)),
        out_shape=jax.ShapeDtypeStruct((depth, bc, d6), F32),
        compiler_params=_params("arbitrary", "arbitrary"),
    )(c, w_ada, b_ada.reshape(depth, 1, d6))


def _mod_spec(mod, layer, which, tm):
    r = mod.shape[3]
    d = mod.shape[4]
    if r == 1:
        return pl.BlockSpec((None, None, None, 1, d), lambda b, t: (layer, which, b, 0, 0))
    return pl.BlockSpec((None, None, None, tm, d), lambda b, t: (layer, which, b, t, 0))


def _qkv_kernel(x_ref, sh_ref, sc_ref, w_ref, k_ref, v_ref, *extra_refs, d, n_kblk):
    h = x_ref[...] * (1.0 + sc_ref[...]) + sh_ref[...]
    hb = h.astype(BF16)
    q = _dot(hb, w_ref[:, 0:d]) * Q_PRESCALE
    k = _dot(hb, w_ref[:, d:2 * d])
    v = _dot(hb, w_ref[:, 2 * d:3 * d])
    k_ref[...] = k
    v_ref[...] = v
    if n_kblk == 0:
        extra_refs[0][...] = q.astype(BF16)
        return
    qt_ref, vt_ref, kh_ref, km_ref = extra_refs
    tm = k.shape[0]
    qt_ref[...] = q.T.astype(BF16)
    vt = v.T
    lane = lax.broadcasted_iota(I32, (tm, LANES), 1)
    pos = lax.broadcasted_iota(I32, (tm, LANES), 0) % MOBA_BLOCK
    key_pos = jnp.where((lane == HEAD_DIM) | (lane == HEAD_DIM + 1), pos, 0).astype(F32)
    ones_row = (lax.broadcasted_iota(I32, (VT_ROWS - HEAD_DIM, tm), 0) == 0).astype(BF16)
    for hd in range(N_HEADS):
        cols = slice(hd * HEAD_DIM, (hd + 1) * HEAD_DIM)
        pair = k[:, (hd // 2) * LANES:(hd // 2 + 1) * LANES]
        if hd % 2:
            pair = pltpu.roll(pair, HEAD_DIM, axis=1)
        kh_ref[hd] = jnp.where(lane < HEAD_DIM, pair, key_pos).astype(BF16)
        vt_ref[hd, 0:HEAD_DIM, :] = vt[cols, :].astype(BF16)
        vt_ref[hd, HEAD_DIM:VT_ROWS, :] = ones_row
        for i in range(n_kblk):
            km_ref[i, hd:hd + 1, :] = jnp.mean(k[i * MOBA_BLOCK:(i + 1) * MOBA_BLOCK, cols], axis=0, keepdims=True)


def _qkv(x, mod, layer, w_qkv_b, tm, prompt):
    b, t, d = x.shape
    n_kblk = tm // MOBA_BLOCK if prompt else 0
    tok = pl.BlockSpec((None, tm, d), lambda bi, ti: (bi, ti, 0))
    out_specs = [tok] * 2
    out_shape = [jax.ShapeDtypeStruct((b, t, d), F32)] * 2
    if prompt:
        out_specs += [pl.BlockSpec((None, d, tm), lambda bi, ti: (bi, 0, ti)),
                      pl.BlockSpec((None, N_HEADS, VT_ROWS, tm), lambda bi, ti: (bi, 0, 0, ti)),
                      pl.BlockSpec((None, N_HEADS, tm, LANES), lambda bi, ti: (bi, 0, ti, 0)),
                      pl.BlockSpec((None, n_kblk, N_HEADS, HEAD_DIM), lambda bi, ti: (bi, ti, 0, 0))]
        out_shape += [jax.ShapeDtypeStruct((b, d, t), BF16),
                      jax.ShapeDtypeStruct((b, N_HEADS, VT_ROWS, t), BF16),
                      jax.ShapeDtypeStruct((b, N_HEADS, t, LANES), BF16),
                      jax.ShapeDtypeStruct((b, t // MOBA_BLOCK, N_HEADS, HEAD_DIM), F32)]
    else:
        out_specs.append(tok)
        out_shape.append(jax.ShapeDtypeStruct((b, t, d), BF16))
    return pl.pallas_call(
        functools.partial(_qkv_kernel, d=d, n_kblk=n_kblk),
        grid=(b, t // tm),
        in_specs=[tok, _mod_spec(mod, layer, 0, tm), _mod_spec(mod, layer, 1, tm),
                  pl.BlockSpec((d, 3 * d), lambda bi, ti: (0, 0))],
        out_specs=out_specs,
        out_shape=out_shape,
        compiler_params=_params("arbitrary", "arbitrary"),
    )(x, mod, mod, w_qkv_b)


def _moba_kernel(offs_ref, qt_ref, k_ref, vt_ref, km_ref, slope_ref, o_ref, sel_scr, score_scr, *, n_blocks):
    hp = pl.program_id(1)
    own = pl.program_id(2)
    tq = MOBA_BLOCK
    blk = lax.broadcasted_iota(I32, (n_blocks, tq), 0)
    blk_f = blk.astype(F32)
    causal = lax.broadcasted_iota(I32, (MOBA_BLOCK, tq), 0) <= lax.broadcasted_iota(I32, (MOBA_BLOCK, tq), 1)
    own_start = pl.multiple_of(own * MOBA_BLOCK, MOBA_BLOCK)
    heads = range(ATT_HEADS)
    rows = [slice(hh * HEAD_DIM, (hh + 1) * HEAD_DIM) for hh in heads]
    block_step = [offs_ref[hp * ATT_HEADS + hh] for hh in heads]
    qt = [jnp.concatenate([qt_ref[rows[hh], :], slope_ref[hh]], axis=0) for hh in heads]

    state = []
    for hh in heads:
        gate = jnp.where(blk < own, _dot(km_ref[hh].astype(BF16), qt_ref[rows[hh], :]), NEG_INF)
        sel = jnp.zeros((n_blocks, tq), F32)
        for _ in range(MOBA_TOPK):
            m = jnp.max(gate, axis=0, keepdims=True)
            first = jnp.min(jnp.where(gate == m, blk_f, float(n_blocks)), axis=0, keepdims=True)
            pick = (blk_f == first) & (m > NEG_INF)
            sel = jnp.where(pick, 1.0, sel)
            gate = jnp.where(pick, NEG_INF, gate)
        sel_scr[hh] = sel

    def scores_into(half, n):
        start = pl.multiple_of(jnp.minimum(n, n_blocks - 1) * MOBA_BLOCK, MOBA_BLOCK)
        for hh in heads:
            score_scr[half, hh] = _dot(k_ref[hh, pl.ds(start, MOBA_BLOCK), :], qt[hh])

    def consume(half, n, state):
        start = pl.multiple_of(jnp.minimum(n, n_blocks - 1) * MOBA_BLOCK, MOBA_BLOCK)
        blocks_back = (own - n).astype(F32)
        new_state = []
        for hh in heads:
            m_i, acc = state[hh]
            off = blocks_back * block_step[hh]
            t = score_scr[half, hh]
            chosen = sel_scr[hh, pl.ds(jnp.minimum(n, n_blocks - 1), 1), :] > 0.0
            m_new = jnp.maximum(m_i, jnp.where(chosen, jnp.max(t, axis=0, keepdims=True) - off, NEG_INF))
            alpha = jnp.exp2(m_i - m_new)
            p = jnp.exp2((t - jnp.where(chosen, m_new + off, float("inf"))).astype(BF16))
            acc_new = alpha * acc + _dot(vt_ref[hh, :, pl.ds(start, MOBA_BLOCK)], p)
            new_state.append((m_new, acc_new))
        return tuple(new_state)

    def body(j, state):
        n = 2 * j
        scores_into(0, n + 1)
        state = consume(1, n, state)
        scores_into(1, n + 2)
        return consume(0, n + 1, state)

    scores_into(0, own)
    scores_into(1, 0)
    state = []
    for hh in heads:
        t = jnp.where(causal, score_scr[0, hh], NEG_INF)
        m_i = jnp.max(t, axis=0, keepdims=True)
        p = jnp.exp2((t - m_i).astype(BF16))
        state.append((m_i, _dot(vt_ref[hh, :, pl.ds(own_start, MOBA_BLOCK)], p)))
    state = lax.fori_loop(0, (own + 1) // 2, body, tuple(state))
    out = jnp.concatenate([acc[0:HEAD_DIM] / acc[HEAD_DIM:HEAD_DIM + 1] for _, acc in state], axis=0)
    o_ref[...] = out.T.astype(BF16)


def _moba_prompt(qt, kh, vt, kmean, slope_rows, block_steps):
    b, d, t = qt.shape
    n_blocks = t // MOBA_BLOCK
    return pl.pallas_call(
        functools.partial(_moba_kernel, n_blocks=n_blocks),
        grid=(b, d // ATT_LANES, n_blocks),
        in_specs=[
            pl.BlockSpec(memory_space=pltpu.SMEM),
            pl.BlockSpec((None, ATT_LANES, MOBA_BLOCK), lambda bi, hp, qi: (bi, hp, qi)),
            pl.BlockSpec((None, ATT_HEADS, t, LANES), lambda bi, hp, qi: (bi, hp, 0, 0)),
            pl.BlockSpec((None, ATT_HEADS, VT_ROWS, t), lambda bi, hp, qi: (bi, hp, 0, 0)),
            pl.BlockSpec((None, ATT_HEADS, n_blocks, HEAD_DIM), lambda bi, hp, qi: (bi, hp, 0, 0)),
            pl.BlockSpec((ATT_HEADS, LANES - HEAD_DIM, MOBA_BLOCK), lambda bi, hp, qi: (hp, 0, 0)),
        ],
        out_specs=pl.BlockSpec((None, MOBA_BLOCK, ATT_LANES), lambda bi, hp, qi: (bi, qi, hp)),
        out_shape=jax.ShapeDtypeStruct((b, t, d), BF16),
        scratch_shapes=[pltpu.VMEM((ATT_HEADS, n_blocks, MOBA_BLOCK), F32),
                        pltpu.VMEM((2, ATT_HEADS, MOBA_BLOCK, MOBA_BLOCK), F32)],
        compiler_params=_params("arbitrary", "arbitrary", "arbitrary"),
    )(block_steps, qt, kh, vt, kmean, slope_rows)


def _dec_attn_kernel(pt_ref, qt_ref, q_ref, kn_ref, vn_ref, bias_ref, *refs, n_pages, page, pps):
    del pt_ref
    kp_refs, vp_refs = refs[:pps], refs[pps:2 * pps]
    o_ref, qb_scr, s_scr, p_scr, acc_scr, pown_scr, l_scr = refs[2 * pps:]
    s_id = pl.program_id(1)
    k_steps = n_pages // pps
    n_past_blocks = n_pages * page // MOBA_BLOCK
    q = q_ref[...]

    def lanes(n):
        return slice(n * MOBA_BLOCK, (n + 1) * MOBA_BLOCK)

    @pl.when(s_id == 0)
    def _():
        qt = qt_ref[...]
        for hd in range(N_HEADS):
            qb_scr[hd] = jnp.broadcast_to(qt[:, hd:hd + 1], (HEAD_DIM, page))

    @pl.when(s_id < k_steps)
    def _():
        for u, kp_ref in enumerate(kp_refs):
            rows = [jnp.sum(kp_ref[hd] * qb_scr[hd], axis=0, keepdims=True) for hd in range(N_HEADS)]
            start = pl.multiple_of((s_id * pps + u) * page, page)
            s_scr[:, pl.ds(start, page)] = jnp.concatenate(rows, axis=0)

    @pl.when(s_id == k_steps - 1)
    def _():
        gates = [jnp.sum(s_scr[:, lanes(n)], axis=1, keepdims=True) / MOBA_BLOCK for n in range(n_past_blocks)]
        sel = [jnp.zeros((N_HEADS, 1), F32) for _ in gates]
        for _ in range(MOBA_TOPK):
            m = functools.reduce(jnp.maximum, gates)
            first = jnp.full((N_HEADS, 1), n_past_blocks, I32)
            for n in reversed(range(n_past_blocks)):
                first = jnp.where(gates[n] == m, n, first)
            for n in range(n_past_blocks):
                pick = (first == n) & (m > NEG_INF)
                sel[n] = jnp.where(pick, 1.0, sel[n])
                gates[n] = jnp.where(pick, NEG_INF, gates[n])

        def logits(n):
            return jnp.where(sel[n] > 0.0, s_scr[:, lanes(n)] + bias_ref[:, lanes(n)], NEG_INF)

        s_own = jnp.sum(q * kn_ref[...], axis=1, keepdims=True)
        m = functools.reduce(jnp.maximum, [jnp.max(logits(n), axis=1, keepdims=True)
                                           for n in range(n_past_blocks)] + [s_own])
        l = jnp.exp2(s_own - m)
        pown_scr[...] = l
        for n in range(n_past_blocks):
            p = jnp.exp2(logits(n) - m)
            l = l + jnp.sum(p, axis=1, keepdims=True)
            p_scr[:, lanes(n)] = p
        l_scr[...] = l
        acc_scr[...] = jnp.zeros_like(acc_scr)

    @pl.when(s_id >= k_steps)
    def _():
        for hd in range(N_HEADS):
            part = acc_scr[hd]
            for u, vp_ref in enumerate(vp_refs):
                start = pl.multiple_of(((s_id - k_steps) * pps + u) * page, page)
                part = part + vp_ref[hd] * p_scr[hd:hd + 1, pl.ds(start, page)]
            acc_scr[hd] = part

    @pl.when(s_id == 2 * k_steps - 1)
    def _():
        ones = jnp.ones((8, page), BF16)
        rows = []
        for hd in range(N_HEADS):
            hi, lo = _split_bf16(acc_scr[hd])
            rows.append((_dot_nt(ones, hi) + _dot_nt(ones, lo))[0:1, :])
        past_part = jnp.concatenate(rows, axis=0)
        o_ref[...] = (past_part + pown_scr[...] * vn_ref[...]) / l_scr[...]


def _moba_decode(q, k_new, v_new, cache_k, cache_v, att_layer, page_table, bias):
    db = q.shape[0]
    n_pages = page_table.shape[1]
    page = cache_k.shape[4]
    past = n_pages * page
    assert past % MOBA_BLOCK == 0 and MOBA_BLOCK % page == 0 and page % LANES == 0
    row = pl.BlockSpec((None, N_HEADS, HEAD_DIM), lambda b, s, pt: (b, 0, 0))
    page_block = (None, None, N_HEADS, HEAD_DIM, page)

    pps = max(u for u in range(1, DECODE_PAGES_PER_STEP + 1) if n_pages % u == 0)
    k_steps = n_pages // pps

    def k_spec(u):
        return pl.BlockSpec(page_block, lambda b, s, pt: (
            att_layer, pt[b * n_pages + jnp.minimum(s, k_steps - 1) * pps + u], 0, 0, 0))

    def v_spec(u):
        return pl.BlockSpec(page_block, lambda b, s, pt: (
            att_layer, pt[b * n_pages + jnp.maximum(s - k_steps, 0) * pps + u], 0, 0, 0))

    return pl.pallas_call(
        functools.partial(_dec_attn_kernel, n_pages=n_pages, page=page, pps=pps),
        grid_spec=pltpu.PrefetchScalarGridSpec(
            num_scalar_prefetch=1,
            grid=(db, 2 * k_steps),
            in_specs=[
                pl.BlockSpec((None, HEAD_DIM, N_HEADS), lambda b, s, pt: (b, 0, 0)),
                row, row, row,
                pl.BlockSpec((N_HEADS, past), lambda b, s, pt: (0, 0)),
                *[k_spec(u) for u in range(pps)],
                *[v_spec(u) for u in range(pps)],
            ],
            out_specs=row,
            scratch_shapes=[
                pltpu.VMEM((N_HEADS, HEAD_DIM, page), F32),
                pltpu.VMEM((N_HEADS, past), F32),
                pltpu.VMEM((N_HEADS, past), F32),
                pltpu.VMEM((N_HEADS, HEAD_DIM, page), F32),
                pltpu.VMEM((N_HEADS, 1), F32),
                pltpu.VMEM((N_HEADS, 1), F32),
            ],
        ),
        out_shape=jax.ShapeDtypeStruct((db, N_HEADS, HEAD_DIM), F32),
        compiler_params=_params("arbitrary", "arbitrary"),
    )(page_table.reshape(-1), q.transpose(0, 2, 1), q, k_new, v_new, bias, *([cache_k] * pps), *([cache_v] * pps))


def _pool_counts(pos, width):
    return jnp.minimum(pos + 1, width).astype(F32)


def _pool_prompt_kernel(x_ref, sh_ref, sc_ref, o_ref, hist_ref, ext, buf_a, buf_b, *, tm, group):
    t = pl.program_id(1)
    rows = POOL_HALO + tm

    @pl.when(t == 0)
    def _():
        ext[0:POOL_HALO, :] = jnp.zeros((POOL_HALO, ext.shape[1]), F32)

    h = x_ref[...] * (1.0 + sc_ref[...]) + sh_ref[...]
    ext[POOL_HALO:rows, :] = h
    d = h.shape[1]
    buf_a[8:rows, :] = ext[8:rows, :] + ext[7:rows - 1, :]
    buf_b[16:rows, group:d] = buf_a[16:rows, group:d] + buf_a[14:rows - 2, group:d]
    buf_a[24:rows, 2 * group:d] = buf_b[24:rows, 2 * group:d] + buf_b[20:rows - 4, 2 * group:d]
    buf_b[32:rows, 3 * group:d] = buf_a[32:rows, 3 * group:d] + buf_a[24:rows - 8, 3 * group:d]
    pos = t * tm + lax.broadcasted_iota(I32, (tm, 1), 0)
    sums = (buf_a, buf_b, buf_a, buf_b)
    for g, width in enumerate(POOL_WINDOWS):
        cols = slice(g * group, (g + 1) * group)
        mean = sums[g][POOL_HALO:rows, cols] / _pool_counts(pos, width)
        o_ref[:, cols] = (mean - h[:, cols]).astype(BF16)
    ext[0:POOL_HALO, :] = h[tm - POOL_HALO:tm, :]

    @pl.when(t == pl.num_programs(1) - 1)
    def _():
        hist_ref[...] = h[tm - POOL_HIST:tm, :]


def _pool_prompt(x, mod, layer, tm):
    b, t, d = x.shape
    tok = pl.BlockSpec((None, tm, d), lambda bi, ti: (bi, ti, 0))
    return pl.pallas_call(
        functools.partial(_pool_prompt_kernel, tm=tm, group=d // len(POOL_WINDOWS)),
        grid=(b, t // tm),
        in_specs=[tok, _mod_spec(mod, layer, 0, tm), _mod_spec(mod, layer, 1, tm)],
        out_specs=[tok, pl.BlockSpec((None, POOL_HIST, d), lambda bi, ti: (bi, 0, 0))],
        out_shape=[jax.ShapeDtypeStruct((b, t, d), BF16), jax.ShapeDtypeStruct((b, POOL_HIST, d), F32)],
        scratch_shapes=[pltpu.VMEM((POOL_HALO + tm, d), F32)] * 3,
        compiler_params=_params("arbitrary", "arbitrary"),
    )(x, mod, mod)


def _pool_decode_kernel(x_ref, sh_ref, sc_ref, hist_ref, o_ref, hist_out_ref, *, group, pos0):
    h = x_ref[...] * (1.0 + sc_ref[...]) + sh_ref[...]
    rows = []
    for b in range(h.shape[0]):
        hb = h[b:b + 1, :]
        parts = []
        for g, width in enumerate(POOL_WINDOWS):
            cols = slice(g * group, (g + 1) * group)
            tail = hist_ref[b, POOL_HIST - (width - 1):POOL_HIST, cols]
            total = jnp.sum(tail, axis=0, keepdims=True) + hb[:, cols]
            parts.append(total / float(min(pos0 + 1, width)) - hb[:, cols])
        rows.append(jnp.concatenate(parts, axis=1))
        hist_out_ref[b, 0:POOL_HIST - 1, :] = hist_ref[b, 1:POOL_HIST, :]
        hist_out_ref[b, POOL_HIST - 1:POOL_HIST, :] = hb
    o_ref[...] = jnp.concatenate(rows, axis=0).astype(BF16)


def _pool_decode(x, mod, layer, hist, pos0):
    _, db, d = x.shape
    bb = 16
    assert db % bb == 0 and pos0 >= POOL_HIST
    tok = pl.BlockSpec((None, bb, d), lambda bi, ti: (0, ti, 0))
    hspec = pl.BlockSpec((bb, POOL_HIST, d), lambda bi, ti: (ti, 0, 0))
    mixed, hist_new = pl.pallas_call(
        functools.partial(_pool_decode_kernel, group=d // len(POOL_WINDOWS), pos0=pos0),
        grid=(1, db // bb),
        in_specs=[tok, _mod_spec(mod, layer, 0, bb), _mod_spec(mod, layer, 1, bb), hspec],
        out_specs=[tok, hspec],
        out_shape=[jax.ShapeDtypeStruct((1, db, d), BF16), jax.ShapeDtypeStruct((db, POOL_HIST, d), F32)],
        compiler_params=_params("arbitrary", "arbitrary"),
    )(x, mod, mod, hist)
    return mixed, hist_new


def _layer_norm(z, g, b):
    mu = jnp.mean(z, axis=-1, keepdims=True)
    zc = z - mu
    var = jnp.mean(zc * zc, axis=-1, keepdims=True)
    return zc * lax.rsqrt(var + LN_EPS) * g + b


def _first_max(vals):
    m = functools.reduce(jnp.maximum, vals)
    idx = jnp.full(m.shape, len(vals) - 1, I32)
    for i in reversed(range(len(vals) - 1)):
        idx = jnp.where(vals[i] == m, i, idx)
    return m, idx


def _top2(vals):
    m0, i0 = _first_max(vals)
    rest = [jnp.where(i0 == i, NEG_INF, v) for i, v in enumerate(vals)]
    m1, i1 = _first_max(rest)
    return m0, i0, m1, i1


def _post_kernel(o_ref, x_ref, g1_ref, sh2_ref, sc2_ref, w_ref, cs_ref, lng_ref, lnb_ref,
                 wr_hi_ref, wr_lo_ref, br_ref, xo_ref, h2_ref, rt_ref, *, alpha):
    out = _dot(o_ref[...], w_ref[...]) * cs_ref[...]
    xn = _layer_norm(alpha * x_ref[...] + g1_ref[...] * out, lng_ref[...], lnb_ref[...])
    xo_ref[...] = xn
    h2 = xn * (1.0 + sc2_ref[...]) + sh2_ref[...]
    h2_ref[...] = h2
    hi, lo = _split_bf16(h2)
    logits = _dot_nt(wr_hi_ref[...], hi) + _dot_nt(wr_hi_ref[...], lo) + _dot_nt(wr_lo_ref[...], hi) + br_ref[...]
    aff = jax.nn.sigmoid(logits)
    rows = [aff[e:e + 1, :] for e in range(N_EXPERTS)]
    gscores = []
    for g in range(N_EXPERT_GROUPS):
        m0, _, m1, _ = _top2(rows[g * EXPERTS_PER_GROUP:(g + 1) * EXPERTS_PER_GROUP])
        gscores.append(m0 + m1)
    _, g_sel = _first_max(gscores)
    in_grp = []
    for i in range(EXPERTS_PER_GROUP):
        v = rows[i]
        for g in range(1, N_EXPERT_GROUPS):
            v = jnp.where(g_sel == g, rows[g * EXPERTS_PER_GROUP + i], v)
        in_grp.append(v)
    w0, i0, w1, i1 = _top2(in_grp)
    denom = w0 + w1
    e0 = (g_sel * EXPERTS_PER_GROUP + i0).astype(F32)
    e1 = (g_sel * EXPERTS_PER_GROUP + i1).astype(F32)
    zero = jnp.zeros_like(w0)
    rt_ref[...] = jnp.concatenate([e0, e1, w0 / denom, w1 / denom, zero, zero, zero, zero], axis=0)


def _post_mixer(o, x, mod, layer, w_b, colscale, ln_g, ln_b, wr_hi, wr_lo, b_router, alpha, tm):
    b, t, d = x.shape
    tok = pl.BlockSpec((None, tm, d), lambda bi, ti: (bi, ti, 0))
    vec = pl.BlockSpec((1, d), lambda bi, ti: (0, 0))
    rtr = pl.BlockSpec((N_EXPERTS, d), lambda bi, ti: (0, 0))
    return pl.pallas_call(
        functools.partial(_post_kernel, alpha=alpha),
        grid=(b, t // tm),
        in_specs=[tok, tok, _mod_spec(mod, layer, 2, tm), _mod_spec(mod, layer, 3, tm), _mod_spec(mod, layer, 4, tm),
                  pl.BlockSpec((d, d), lambda bi, ti: (0, 0)), vec, vec, vec, rtr, rtr,
                  pl.BlockSpec((N_EXPERTS, 1), lambda bi, ti: (0, 0))],
        out_specs=[tok, tok, pl.BlockSpec((None, 8, tm), lambda bi, ti: (bi, 0, ti))],
        out_shape=[jax.ShapeDtypeStruct((b, t, d), F32), jax.ShapeDtypeStruct((b, t, d), F32),
                   jax.ShapeDtypeStruct((b, 8, t), F32)],
        compiler_params=_params("arbitrary", "arbitrary"),
    )(o, x, mod, mod, mod, w_b, colscale, ln_g, ln_b, wr_hi, wr_lo, b_router)


def _moe_kernel(be_ref, nact_ref, nval_ref, src_ref, src_next_ref, dst_ref, gate_ref, h_hbm, wg_ref, wu_ref, wd_ref,
                y_hbm, xg, yb, wgb, wub, wdb, sem_in, sem_out):
    i = pl.program_id(0)
    tb = xg.shape[1]
    n_active = nact_ref[0]
    slot = lax.rem(i, 2)

    def start_gather(idx_ref, to_slot):
        def body(g, carry):
            for u in range(DMA_UNROLL):
                r = g * DMA_UNROLL + u
                pltpu.make_async_copy(h_hbm.at[pl.ds(idx_ref[0, 0, r], 1), :], xg.at[to_slot, pl.ds(r, 1), :],
                                      sem_in.at[to_slot]).start()
            return carry
        lax.fori_loop(0, tb // DMA_UNROLL, body, 0)

    def scatter(r):
        return pltpu.make_async_copy(yb.at[pl.ds(r, 1), :], y_hbm.at[pl.ds(dst_ref[0, 0, r], 1), :], sem_out)

    def first_rows(n, fn):
        def group(g, carry):
            for u in range(DMA_UNROLL):
                fn(g * DMA_UNROLL + u)
            return carry
        groups = n // DMA_UNROLL
        lax.fori_loop(0, groups, group, 0)

        def single(r, carry):
            fn(r)
            return carry
        lax.fori_loop(groups * DMA_UNROLL, n, single, 0)

    @pl.when(i < n_active)
    def _():
        @pl.when(i == 0)
        def _():
            start_gather(src_ref, 0)

        changed = jnp.logical_or(i == 0, be_ref[i] != be_ref[jnp.maximum(i - 1, 0)])

        @pl.when(changed)
        def _():
            wgb[...] = wg_ref[...].astype(BF16)
            wub[...] = wu_ref[...].astype(BF16)
            wdb[...] = wd_ref[...].astype(BF16)

        pltpu.make_async_copy(h_hbm.at[pl.ds(0, tb), :], xg.at[slot], sem_in.at[slot]).wait()

        @pl.when(i > 0)
        def _():
            first_rows(nval_ref[jnp.maximum(i - 1, 0)], lambda r: scatter(r).wait())

        def expert(cur):
            x = xg[cur].astype(BF16)
            a = _dot(x, wgb[...])
            a = a * jax.nn.sigmoid(a) * _dot(x, wub[...])
            yb[...] = _dot(a.astype(BF16), wdb[...]) * gate_ref[...]

        for cur in range(2):
            @pl.when(jnp.logical_and(slot == cur, i + 1 < n_active))
            def _(cur=cur):
                for r in range(tb):
                    pltpu.make_async_copy(h_hbm.at[pl.ds(src_next_ref[0, 0, r], 1), :],
                                          xg.at[1 - cur, pl.ds(r, 1), :], sem_in.at[1 - cur]).start()
                expert(cur)

            @pl.when(jnp.logical_and(slot == cur, i + 1 >= n_active))
            def _(cur=cur):
                expert(cur)

        first_rows(nval_ref[i], lambda r: scatter(r).start())

        @pl.when(i == n_active - 1)
        def _():
            first_rows(nval_ref[i], lambda r: scatter(r).wait())


def _moe_experts(h2_flat, blk_expert, n_active, n_valid, src_tok, dst_row, gate_rows, w_gate, w_up, w_down, layer,
                 n_out_rows):
    n_rows = src_tok.shape[0]
    tb = MOE_ROWS
    n_blk = n_rows // tb
    d = h2_flat.shape[1]
    de = w_gate.shape[3]
    idx = pl.BlockSpec((1, 1, tb), lambda i, *_: (i, 0, 0), memory_space=pltpu.SMEM)
    idx_next = pl.BlockSpec((1, 1, tb), lambda i, *_: (jnp.minimum(i + 1, n_blk - 1), 0, 0), memory_space=pltpu.SMEM)
    src3 = src_tok.reshape(n_blk, 1, tb)
    return pl.pallas_call(
        _moe_kernel,
        grid_spec=pltpu.PrefetchScalarGridSpec(
            num_scalar_prefetch=3,
            grid=(n_blk,),
            in_specs=[
                idx, idx_next, idx,
                pl.BlockSpec((tb, 1), lambda i, *_: (i, 0)),
                pl.BlockSpec(memory_space=pl.ANY),
                pl.BlockSpec((None, None, d, de), lambda i, be, *_: (layer, be[i], 0, 0)),
                pl.BlockSpec((None, None, d, de), lambda i, be, *_: (layer, be[i], 0, 0)),
                pl.BlockSpec((None, None, de, d), lambda i, be, *_: (layer, be[i], 0, 0)),
            ],
            out_specs=pl.BlockSpec(memory_space=pl.ANY),
            scratch_shapes=[
                pltpu.VMEM((2, tb, d), F32), pltpu.VMEM((tb, d), F32),
                pltpu.VMEM((d, de), BF16), pltpu.VMEM((d, de), BF16), pltpu.VMEM((de, d), BF16),
                pltpu.SemaphoreType.DMA((2,)), pltpu.SemaphoreType.DMA(()),
            ],
        ),
        out_shape=jax.ShapeDtypeStruct((n_out_rows, d), F32),
        compiler_params=_params("arbitrary"),
    )(blk_expert, n_active, n_valid, src3, src3, dst_row.reshape(n_blk, 1, tb),
      gate_rows.reshape(n_rows, 1), h2_flat, w_gate, w_up, w_down)


def _route_tables(rt, n_tok):
    tb = MOE_ROWS
    m = 2 * n_tok
    e = jnp.stack([rt[:, 0, :], rt[:, 1, :]], axis=-1).reshape(m).astype(I32)
    g = jnp.stack([rt[:, 2, :], rt[:, 3, :]], axis=-1).reshape(m)
    onehot = (e[:, None] == jnp.arange(N_EXPERTS, dtype=I32)[None, :]).astype(I32)
    counts = onehot.sum(0)
    rank = jnp.sum((jnp.cumsum(onehot, axis=0) - onehot) * onehot, axis=1)
    padded = (counts + tb - 1) // tb * tb
    pad_ends = jnp.cumsum(padded)
    pad_starts = pad_ends - padded
    dest = pad_starts[e] + rank
    n_blk = -(-(m + N_EXPERTS * (tb - 1)) // tb)
    n_rows = n_blk * tb
    pair = jnp.arange(m, dtype=I32)
    packed = jnp.stack([pair // 2, (pair % 2) * n_tok + pair // 2, lax.bitcast_convert_type(g, I32)], axis=1)
    table = jnp.zeros((n_rows, 3), I32).at[dest].set(packed)
    src_tok, dst_row, gate_rows = table[:, 0], table[:, 1], lax.bitcast_convert_type(table[:, 2], F32)
    blk_start = jnp.arange(n_blk, dtype=I32) * tb
    blk_expert = jnp.minimum(jnp.sum(blk_start[:, None] >= pad_ends[None, :], axis=1), N_EXPERTS - 1).astype(I32)
    n_active = (pad_ends[-1] // tb).astype(I32).reshape(1)
    n_valid = jnp.clip(counts[blk_expert] - (blk_start - pad_starts[blk_expert]), 0, tb).astype(I32)
    return blk_expert, n_active, n_valid, src_tok, dst_row, gate_rows, m


def _final_kernel(x_ref, y0_ref, y1_ref, g2_ref, lng_ref, lnb_ref, o_ref, *, alpha):
    ffn = y0_ref[...] + y1_ref[...]
    o_ref[...] = _layer_norm(alpha * x_ref[...] + g2_ref[...] * ffn, lng_ref[...], lnb_ref[...])


def _final_ln(x, y_slots, mod, layer, ln_g, ln_b, alpha, tm):
    b, t, d = x.shape
    tok = pl.BlockSpec((None, tm, d), lambda bi, ti: (bi, ti, 0))
    vec = pl.BlockSpec((1, d), lambda bi, ti: (0, 0))
    tiles = t // tm

    def slot(k):
        return pl.BlockSpec((None, tm, d), lambda bi, ti: (k, bi * tiles + ti, 0))

    return pl.pallas_call(
        functools.partial(_final_kernel, alpha=alpha),
        grid=(b, tiles),
        in_specs=[tok, slot(0), slot(1), _mod_spec(mod, layer, 5, tm), vec, vec],
        out_specs=tok,
        out_shape=jax.ShapeDtypeStruct((b, t, d), F32),
        compiler_params=_params("arbitrary", "arbitrary"),
    )(x, y_slots, y_slots, mod, ln_g, ln_b)


def _run_trunk(x, mod, pos0, cache_k, cache_v, page_table, pool_state, prm, tm):
    b, t, d = x.shape
    depth = prm["depth"]
    alpha = (2.0 * depth) ** 0.25
    decode = cache_k is not None
    new_k, new_v, new_pool = [], [], []
    for i in range(depth):
        j = i // 2
        if i % 2 == 0:
            if decode:
                k, v, q = _qkv(x, mod, i, prm["w_qkv_b"][j], tm, prompt=False)
                per_head = (t, N_HEADS, HEAD_DIM)
                o = _moba_decode(q.astype(F32).reshape(per_head), k.reshape(per_head), v.reshape(per_head),
                                 cache_k, cache_v, j, page_table, prm["bias_decode"])
                o = o.reshape(1, t, d).astype(BF16)
            else:
                k, v, qt, vt, kh, km = _qkv(x, mod, i, prm["w_qkv_b"][j], tm, prompt=True)
                o = _moba_prompt(qt, kh, vt, km.transpose(0, 2, 1, 3), prm["slope_rows"], prm["block_steps"])
            new_k.append(k)
            new_v.append(v)
            w_b, colscale = prm["w_o_b"][j], prm["ones"]
        else:
            if decode:
                o, hist_new = _pool_decode(x, mod, i, pool_state[j], pos0)
            else:
                o, hist_new = _pool_prompt(x, mod, i, tm)
            new_pool.append(hist_new)
            w_b, colscale = prm["w_pool_b"][j], prm["pool_scale"][j]
        x, h2, rt = _post_mixer(o, x, mod, i, w_b, colscale, prm["ln_g"][i, 0], prm["ln_b"][i, 0],
                                prm["wr_hi"], prm["wr_lo"], prm["b_router"], alpha, tm)
        n_tok = b * t
        blk_expert, n_active, n_valid, src_tok, dst_row, gate_rows, n_out_rows = _route_tables(rt, n_tok)
        y = _moe_experts(h2.reshape(n_tok, d), blk_expert, n_active, n_valid, src_tok, dst_row, gate_rows,
                         prm["w_gate"], prm["w_up"], prm["w_down"], i, n_out_rows)
        x = _final_ln(x, y.reshape(2, n_tok, d), mod, i, prm["ln_g"][i, 1], prm["ln_b"][i, 1], alpha, tm)
    return x, new_k, new_v, new_pool


def kernel(x_prompt, x_sample, cache_k, cache_v, state_pool, page_table, c_prompt, c_sample, w_ada, b_ada, ln_g, ln_b, w_qkv, w_o, w_pool, pool_scale, w_router, b_router, w_gate, w_up, w_down):
    bp, seq, d = x_prompt.shape
    db, dec_seq, _ = x_sample.shape
    depth = w_ada.shape[0]
    assert dec_seq == 1 and d == N_HEADS * HEAD_DIM and seq % MOBA_BLOCK == 0
    pos0 = page_table.shape[1] * cache_k.shape[2]

    c_all = jnp.concatenate([c_prompt, c_sample], axis=0)
    pad = (-c_all.shape[0]) % 8
    c_all = jnp.pad(c_all, ((0, pad), (0, 0)))
    mod = _adaln(c_all, w_ada, b_ada).reshape(depth, c_all.shape[0], 6, d).transpose(0, 2, 1, 3)
    mod_p = mod[:, :, :bp].reshape(depth, 6, bp, 1, d)
    mod_s = mod[:, :, bp:bp + db].reshape(depth, 6, 1, db, d)

    group = d // len(POOL_WINDOWS)
    w_pool_dense = jnp.zeros((w_pool.shape[0], d, d), F32)
    for g in range(len(POOL_WINDOWS)):
        w_pool_dense = w_pool_dense.at[:, g * group:(g + 1) * group, g * group:(g + 1) * group].set(w_pool[:, g])
    wr_hi, wr_lo = _split_bf16(w_router.T)
    slopes2 = LOG2E * 2.0 ** (-(8.0 / N_HEADS) * jnp.arange(1, N_HEADS + 1, dtype=F32))
    c_hi, c_lo = _split_bf16(slopes2)
    slope_rows = jnp.zeros((N_HEADS, LANES - HEAD_DIM, MOBA_BLOCK), BF16)
    slope_rows = slope_rows.at[:, 0, :].set(c_hi[:, None]).at[:, 1, :].set(c_lo[:, None])
    bias_decode = -slopes2[:, None] * (pos0 - jnp.arange(pos0, dtype=I32)).astype(F32)[None, :]
    cache_kt = cache_k.transpose(0, 1, 3, 4, 2)
    cache_vt = cache_v.transpose(0, 1, 3, 4, 2)
    prm = dict(
        depth=depth,
        w_qkv_b=w_qkv.astype(BF16), w_o_b=w_o.astype(BF16), w_pool_b=w_pool_dense.astype(BF16),
        pool_scale=pool_scale.reshape(-1, 1, d), ones=jnp.ones((1, d), F32),
        ln_g=ln_g.reshape(depth, 2, 1, d), ln_b=ln_b.reshape(depth, 2, 1, d),
        wr_hi=wr_hi, wr_lo=wr_lo, b_router=b_router.reshape(N_EXPERTS, 1),
        w_gate=w_gate, w_up=w_up, w_down=w_down,
        slope_rows=slope_rows, block_steps=slopes2 * MOBA_BLOCK, bias_decode=bias_decode,
    )

    y_p, k_p, v_p, pool_p = _run_trunk(x_prompt, mod_p, 0, None, None, None, None, prm, tm=MOBA_BLOCK)
    y_s, k_s, v_s, pool_s = _run_trunk(
        x_sample.reshape(1, db, d), mod_s, pos0,
        cache_kt, cache_vt, page_table, state_pool, prm, tm=db)

    def heads(ts, lead):
        return jnp.stack(ts).reshape(len(ts), *lead, N_HEADS, HEAD_DIM)

    return (y_p, y_s.reshape(db, 1, d),
            heads(k_p, (bp, seq)), heads(v_p, (bp, seq)), jnp.stack(pool_p),
            heads(k_s, (db, 1)), heads(v_s, (db, 1)), jnp.stack(pool_s))
```

```python
import functools

import jax
import jax.numpy as jnp
from jax import lax
from jax.experimental import pallas as pl
from jax.experimental.pallas import tpu as pltpu

F32 = jnp.float32
BF16 = jnp.bfloat16
I32 = jnp.int32

N_HEADS = 16
HEAD_DIM = 64
MOBA_BLOCK = 256
MOBA_TOPK = 3
POOL_WINDOWS = (2, 4, 8, 16)
POOL_HIST = max(POOL_WINDOWS) - 1
POOL_HALO = 32
N_EXPERTS = 16
N_EXPERT_GROUPS = 4
EXPERTS_PER_GROUP = N_EXPERTS // N_EXPERT_GROUPS
LN_EPS = 1e-5
LANES = 128
ATT_HEADS = 4
ATT_LANES = ATT_HEADS * HEAD_DIM
VT_ROWS = HEAD_DIM + 16
MOE_ROWS = 256
DMA_UNROLL = 8
DECODE_PAGES_PER_STEP = 8
VMEM_LIMIT = 48 * 1024 * 1024
NEG_INF = float("-inf")
LOG2E = 1.4426950408889634
Q_PRESCALE = HEAD_DIM ** -0.5 * LOG2E


def _dot(a, b):
    return jnp.dot(a, b, preferred_element_type=F32)


def _dot_nt(a, b):
    return lax.dot_general(a, b, (((1,), (1,)), ((), ())), preferred_element_type=F32)


def _split_bf16(x):
    hi = x.astype(BF16)
    lo = (x - hi.astype(F32)).astype(BF16)
    return hi, lo


def _params(*sem):
    return pltpu.CompilerParams(dimension_semantics=sem, vmem_limit_bytes=VMEM_LIMIT)


def _adaln_kernel(c_ref, w_ref, b_ref, o_ref):
    c = c_ref[...]
    s_hi, s_lo = _split_bf16(c * jax.nn.sigmoid(c))
    w_hi, w_lo = _split_bf16(w_ref[...])
    o_ref[...] = _dot(s_hi, w_hi) + _dot(s_lo, w_hi) + _dot(s_hi, w_lo) + b_ref[...]


def _adaln(c, w_ada, b_ada):
    depth, d, d6 = w_ada.shape
    bc = c.shape[0]
    tn = 1024
    return pl.pallas_call(
        _adaln_kernel,
        grid=(depth, d6 // tn),
        in_specs=[
            pl.BlockSpec((bc, d), lambda l, j: (0, 0)),
            pl.BlockSpec((None, d, tn), lambda l, j: (l, 0, j)),
            pl.BlockSpec((None, 1, tn), lambda l, j: (l, 0, j)),
        ],
        out_specs=pl.BlockSpec((None, bc, tn), lambda l, j: (l, 0, j)),
        out_shape=jax.ShapeDtypeStruct((depth, bc, d6), F32),
        compiler_params=_params("arbitrary", "arbitrary"),
    )(c, w_ada, b_ada.reshape(depth, 1, d6))


def _mod_spec(mod, layer, which, tm):
    r = mod.shape[3]
    d = mod.shape[4]
    if r == 1:
        return pl.BlockSpec((None, None, None, 1, d), lambda b, t: (layer, which, b, 0, 0))
    return pl.BlockSpec((None, None, None, tm, d), lambda b, t: (layer, which, b, t, 0))


def _qkv_kernel(x_ref, sh_ref, sc_ref, w_ref, k_ref, v_ref, *extra_refs, d, n_kblk):
    h = x_ref[...] * (1.0 + sc_ref[...]) + sh_ref[...]
    hb = h.astype(BF16)
    q = _dot(hb, w_ref[:, 0:d]) * Q_PRESCALE
    k = _dot(hb, w_ref[:, d:2 * d])
    v = _dot(hb, w_ref[:, 2 * d:3 * d])
    k_ref[...] = k
    v_ref[...] = v
    if n_kblk == 0:
        extra_refs[0][...] = q.astype(BF16)
        return
    qt_ref, vt_ref, kh_ref, km_ref = extra_refs
    tm = k.shape[0]
    qt_ref[...] = q.T.astype(BF16)
    vt = v.T
    lane = lax.broadcasted_iota(I32, (tm, LANES), 1)
    pos = lax.broadcasted_iota(I32, (tm, LANES), 0) % MOBA_BLOCK
    key_pos = jnp.where((lane == HEAD_DIM) | (lane == HEAD_DIM + 1), pos, 0).astype(F32)
    ones_row = (lax.broadcasted_iota(I32, (VT_ROWS - HEAD_DIM, tm), 0) == 0).astype(BF16)
    for hd in range(N_HEADS):
        cols = slice(hd * HEAD_DIM, (hd + 1) * HEAD_DIM)
        pair = k[:, (hd // 2) * LANES:(hd // 2 + 1) * LANES]
        if hd % 2:
            pair = pltpu.roll(pair, HEAD_DIM, axis=1)
        kh_ref[hd] = jnp.where(lane < HEAD_DIM, pair, key_pos).astype(BF16)
        vt_ref[hd, 0:HEAD_DIM, :] = vt[cols, :].astype(BF16)
        vt_ref[hd, HEAD_DIM:VT_ROWS, :] = ones_row
        for i in range(n_kblk):
            km_ref[i, hd:hd + 1, :] = jnp.mean(k[i * MOBA_BLOCK:(i + 1) * MOBA_BLOCK, cols], axis=0, keepdims=True)


def _qkv(x, mod, layer, w_qkv_b, tm, prompt):
    b, t, d = x.shape
    n_kblk = tm // MOBA_BLOCK if prompt else 0
    tok = pl.BlockSpec((None, tm, d), lambda bi, ti: (bi, ti, 0))
    out_specs = [tok] * 2
    out_shape = [jax.ShapeDtypeStruct((b, t, d), F32)] * 2
    if prompt:
        out_specs += [pl.BlockSpec((None, d, tm), lambda bi, ti: (bi, 0, ti)),
                      pl.BlockSpec((None, N_HEADS, VT_ROWS, tm), lambda bi, ti: (bi, 0, 0, ti)),
                      pl.BlockSpec((None, N_HEADS, tm, LANES), lambda bi, ti: (bi, 0, ti, 0)),
                      pl.BlockSpec((None, n_kblk, N_HEADS, HEAD_DIM), lambda bi, ti: (bi, ti, 0, 0))]
        out_shape += [jax.ShapeDtypeStruct((b, d, t), BF16),
                      jax.ShapeDtypeStruct((b, N_HEADS, VT_ROWS, t), BF16),
                      jax.ShapeDtypeStruct((b, N_HEADS, t, LANES), BF16),
                      jax.ShapeDtypeStruct((b, t // MOBA_BLOCK, N_HEADS, HEAD_DIM), F32)]
    else:
        out_specs.append(tok)
        out_shape.append(jax.ShapeDtypeStruct((b, t, d), BF16))
    return pl.pallas_call(
        functools.partial(_qkv_kernel, d=d, n_kblk=n_kblk),
        grid=(b, t // tm),
        in_specs=[tok, _mod_spec(mod, layer, 0, tm), _mod_spec(mod, layer, 1, tm),
                  pl.BlockSpec((d, 3 * d), lambda bi, ti: (0, 0))],
        out_specs=out_specs,
        out_shape=out_shape,
        compiler_params=_params("arbitrary", "arbitrary"),
    )(x, mod, mod, w_qkv_b)


def _moba_kernel(offs_ref, qt_ref, k_ref, vt_ref, km_ref, slope_ref, o_ref, sel_scr, score_scr, *, n_blocks):
    hp = pl.program_id(1)
    own = pl.program_id(2)
    tq = MOBA_BLOCK
    blk = lax.broadcasted_iota(I32, (n_blocks, tq), 0)
    blk_f = blk.astype(F32)
    causal = lax.broadcasted_iota(I32, (MOBA_BLOCK, tq), 0) <= lax.broadcasted_iota(I32, (MOBA_BLOCK, tq), 1)
    own_start = pl.multiple_of(own * MOBA_BLOCK, MOBA_BLOCK)
    heads = range(ATT_HEADS)
    rows = [slice(hh * HEAD_DIM, (hh + 1) * HEAD_DIM) for hh in heads]
    block_step = [offs_ref[hp * ATT_HEADS + hh] for hh in heads]
    qt = [jnp.concatenate([qt_ref[rows[hh], :], slope_ref[hh]], axis=0) for hh in heads]

    state = []
    for hh in heads:
        gate = jnp.where(blk < own, _dot(km_ref[hh].astype(BF16), qt_ref[rows[hh], :]), NEG_INF)
        sel = jnp.zeros((n_blocks, tq), F32)
        for _ in range(MOBA_TOPK):
            m = jnp.max(gate, axis=0, keepdims=True)
            first = jnp.min(jnp.where(gate == m, blk_f, float(n_blocks)), axis=0, keepdims=True)
            pick = (blk_f == first) & (m > NEG_INF)
            sel = jnp.where(pick, 1.0, sel)
            gate = jnp.where(pick, NEG_INF, gate)
        sel_scr[hh] = sel

    def scores_into(half, n):
        start = pl.multiple_of(jnp.minimum(n, n_blocks - 1) * MOBA_BLOCK, MOBA_BLOCK)
        for hh in heads:
            score_scr[half, hh] = _dot(k_ref[hh, pl.ds(start, MOBA_BLOCK), :], qt[hh])

    def consume(half, n, state):
        start = pl.multiple_of(jnp.minimum(n, n_blocks - 1) * MOBA_BLOCK, MOBA_BLOCK)
        blocks_back = (own - n).astype(F32)
        new_state = []
        for hh in heads:
            m_i, acc = state[hh]
            off = blocks_back * block_step[hh]
            t = score_scr[half, hh]
            chosen = sel_scr[hh, pl.ds(jnp.minimum(n, n_blocks - 1), 1), :] > 0.0
            m_new = jnp.maximum(m_i, jnp.where(chosen, jnp.max(t, axis=0, keepdims=True) - off, NEG_INF))
            alpha = jnp.exp2(m_i - m_new)
            p = jnp.exp2((t - jnp.where(chosen, m_new + off, float("inf"))).astype(BF16))
            acc_new = alpha * acc + _dot(vt_ref[hh, :, pl.ds(start, MOBA_BLOCK)], p)
            new_state.append((m_new, acc_new))
        return tuple(new_state)

    def body(j, state):
        n = 2 * j
        scores_into(0, n + 1)
        state = consume(1, n, state)
        scores_into(1, n + 2)
        return consume(0, n + 1, state)

    scores_into(0, own)
    scores_into(1, 0)
    state = []
    for hh in heads:
        t = jnp.where(causal, score_scr[0, hh], NEG_INF)
        m_i = jnp.max(t, axis=0, keepdims=True)
        p = jnp.exp2((t - m_i).astype(BF16))
        state.append((m_i, _dot(vt_ref[hh, :, pl.ds(own_start, MOBA_BLOCK)], p)))
    state = lax.fori_loop(0, (own + 1) // 2, body, tuple(state))
    out = jnp.concatenate([acc[0:HEAD_DIM] / acc[HEAD_DIM:HEAD_DIM + 1] for _, acc in state], axis=0)
    o_ref[...] = out.T.astype(BF16)


def _moba_prompt(qt, kh, vt, kmean, slope_rows, block_steps):
    b, d, t = qt.shape
    n_blocks = t // MOBA_BLOCK
    return pl.pallas_call(
        functools.partial(_moba_kernel, n_blocks=n_blocks),
        grid=(b, d // ATT_LANES, n_blocks),
        in_specs=[
            pl.BlockSpec(memory_space=pltpu.SMEM),
            pl.BlockSpec((None, ATT_LANES, MOBA_BLOCK), lambda bi, hp, qi: (bi, hp, qi)),
            pl.BlockSpec((None, ATT_HEADS, t, LANES), lambda bi, hp, qi: (bi, hp, 0, 0)),
            pl.BlockSpec((None, ATT_HEADS, VT_ROWS, t), lambda bi, hp, qi: (bi, hp, 0, 0)),
            pl.BlockSpec((None, ATT_HEADS, n_blocks, HEAD_DIM), lambda bi, hp, qi: (bi, hp, 0, 0)),
            pl.BlockSpec((ATT_HEADS, LANES - HEAD_DIM, MOBA_BLOCK), lambda bi, hp, qi: (hp, 0, 0)),
        ],
        out_specs=pl.BlockSpec((None, MOBA_BLOCK, ATT_LANES), lambda bi, hp, qi: (bi, qi, hp)),
        out_shape=jax.ShapeDtypeStruct((b, t, d), BF16),
        scratch_shapes=[pltpu.VMEM((ATT_HEADS, n_blocks, MOBA_BLOCK), F32),
                        pltpu.VMEM((2, ATT_HEADS, MOBA_BLOCK, MOBA_BLOCK), F32)],
        compiler_params=_params("arbitrary", "arbitrary", "arbitrary"),
    )(block_steps, qt, kh, vt, kmean, slope_rows)


def _dec_attn_kernel(pt_ref, qt_ref, q_ref, kn_ref, vn_ref, bias_ref, *refs, n_pages, page, pps):
    del pt_ref
    kp_refs, vp_refs = refs[:pps], refs[pps:2 * pps]
    o_ref, qb_scr, s_scr, p_scr, acc_scr, pown_scr, l_scr = refs[2 * pps:]
    s_id = pl.program_id(1)
    k_steps = n_pages // pps
    n_past_blocks = n_pages * page // MOBA_BLOCK
    q = q_ref[...]

    def lanes(n):
        return slice(n * MOBA_BLOCK, (n + 1) * MOBA_BLOCK)

    @pl.when(s_id == 0)
    def _():
        qt = qt_ref[...]
        for hd in range(N_HEADS):
            qb_scr[hd] = jnp.broadcast_to(qt[:, hd:hd + 1], (HEAD_DIM, page))

    @pl.when(s_id < k_steps)
    def _():
        for u, kp_ref in enumerate(kp_refs):
            rows = [jnp.sum(kp_ref[hd] * qb_scr[hd], axis=0, keepdims=True) for hd in range(N_HEADS)]
            start = pl.multiple_of((s_id * pps + u) * page, page)
            s_scr[:, pl.ds(start, page)] = jnp.concatenate(rows, axis=0)

    @pl.when(s_id == k_steps - 1)
    def _():
        gates = [jnp.sum(s_scr[:, lanes(n)], axis=1, keepdims=True) / MOBA_BLOCK for n in range(n_past_blocks)]
        sel = [jnp.zeros((N_HEADS, 1), F32) for _ in gates]
        for _ in range(MOBA_TOPK):
            m = functools.reduce(jnp.maximum, gates)
            first = jnp.full((N_HEADS, 1), n_past_blocks, I32)
            for n in reversed(range(n_past_blocks)):
                first = jnp.where(gates[n] == m, n, first)
            for n in range(n_past_blocks):
                pick = (first == n) & (m > NEG_INF)
                sel[n] = jnp.where(pick, 1.0, sel[n])
                gates[n] = jnp.where(pick, NEG_INF, gates[n])

        def logits(n):
            return jnp.where(sel[n] > 0.0, s_scr[:, lanes(n)] + bias_ref[:, lanes(n)], NEG_INF)

        s_own = jnp.sum(q * kn_ref[...], axis=1, keepdims=True)
        m = functools.reduce(jnp.maximum, [jnp.max(logits(n), axis=1, keepdims=True)
                                           for n in range(n_past_blocks)] + [s_own])
        l = jnp.exp2(s_own - m)
        pown_scr[...] = l
        for n in range(n_past_blocks):
            p = jnp.exp2(logits(n) - m)
            l = l + jnp.sum(p, axis=1, keepdims=True)
            p_scr[:, lanes(n)] = p
        l_scr[...] = l
        acc_scr[...] = jnp.zeros_like(acc_scr)

    @pl.when(s_id >= k_steps)
    def _():
        for hd in range(N_HEADS):
            part = acc_scr[hd]
            for u, vp_ref in enumerate(vp_refs):
                start = pl.multiple_of(((s_id - k_steps) * pps + u) * page, page)
                part = part + vp_ref[hd] * p_scr[hd:hd + 1, pl.ds(start, page)]
            acc_scr[hd] = part

    @pl.when(s_id == 2 * k_steps - 1)
    def _():
        ones = jnp.ones((8, page), BF16)
        rows = []
        for hd in range(N_HEADS):
            hi, lo = _split_bf16(acc_scr[hd])
            rows.append((_dot_nt(ones, hi) + _dot_nt(ones, lo))[0:1, :])
        past_part = jnp.concatenate(rows, axis=0)
        o_ref[...] = (past_part + pown_scr[...] * vn_ref[...]) / l_scr[...]


def _moba_decode(q, k_new, v_new, cache_k, cache_v, att_layer, page_table, bias):
    db = q.shape[0]
    n_pages = page_table.shape[1]
    page = cache_k.shape[4]
    past = n_pages * page
    assert past % MOBA_BLOCK == 0 and MOBA_BLOCK % page == 0 and page % LANES == 0
    row = pl.BlockSpec((None, N_HEADS, HEAD_DIM), lambda b, s, pt: (b, 0, 0))
    page_block = (None, None, N_HEADS, HEAD_DIM, page)

    pps = max(u for u in range(1, DECODE_PAGES_PER_STEP + 1) if n_pages % u == 0)
    k_steps = n_pages // pps

    def k_spec(u):
        return pl.BlockSpec(page_block, lambda b, s, pt: (
            att_layer, pt[b * n_pages + jnp.minimum(s, k_steps - 1) * pps + u], 0, 0, 0))

    def v_spec(u):
        return pl.BlockSpec(page_block, lambda b, s, pt: (
            att_layer, pt[b * n_pages + jnp.maximum(s - k_steps, 0) * pps + u], 0, 0, 0))

    return pl.pallas_call(
        functools.partial(_dec_attn_kernel, n_pages=n_pages, page=page, pps=pps),
        grid_spec=pltpu.PrefetchScalarGridSpec(
            num_scalar_prefetch=1,
            grid=(db, 2 * k_steps),
            in_specs=[
                pl.BlockSpec((None, HEAD_DIM, N_HEADS), lambda b, s, pt: (b, 0, 0)),
                row, row, row,
                pl.BlockSpec((N_HEADS, past), lambda b, s, pt: (0, 0)),
                *[k_spec(u) for u in range(pps)],
                *[v_spec(u) for u in range(pps)],
            ],
            out_specs=row,
            scratch_shapes=[
                pltpu.VMEM((N_HEADS, HEAD_DIM, page), F32),
                pltpu.VMEM((N_HEADS, past), F32),
                pltpu.VMEM((N_HEADS, past), F32),
                pltpu.VMEM((N_HEADS, HEAD_DIM, page), F32),
                pltpu.VMEM((N_HEADS, 1), F32),
                pltpu.VMEM((N_HEADS, 1), F32),
            ],
        ),
        out_shape=jax.ShapeDtypeStruct((db, N_HEADS, HEAD_DIM), F32),
        compiler_params=_params("arbitrary", "arbitrary"),
    )(page_table.reshape(-1), q.transpose(0, 2, 1), q, k_new, v_new, bias, *([cache_k] * pps), *([cache_v] * pps))


def _pool_counts(pos, width):
    return jnp.minimum(pos + 1, width).astype(F32)


def _pool_prompt_kernel(x_ref, sh_ref, sc_ref, o_ref, hist_ref, ext, buf_a, buf_b, *, tm, group):
    t = pl.program_id(1)
    rows = POOL_HALO + tm

    @pl.when(t == 0)
    def _():
        ext[0:POOL_HALO, :] = jnp.zeros((POOL_HALO, ext.shape[1]), F32)

    h = x_ref[...] * (1.0 + sc_ref[...]) + sh_ref[...]
    ext[POOL_HALO:rows, :] = h
    d = h.shape[1]
    buf_a[8:rows, :] = ext[8:rows, :] + ext[7:rows - 1, :]
    buf_b[16:rows, group:d] = buf_a[16:rows, group:d] + buf_a[14:rows - 2, group:d]
    buf_a[24:rows, 2 * group:d] = buf_b[24:rows, 2 * group:d] + buf_b[20:rows - 4, 2 * group:d]
    buf_b[32:rows, 3 * group:d] = buf_a[32:rows, 3 * group:d] + buf_a[24:rows - 8, 3 * group:d]
    pos = t * tm + lax.broadcasted_iota(I32, (tm, 1), 0)
    sums = (buf_a, buf_b, buf_a, buf_b)
    for g, width in enumerate(POOL_WINDOWS):
        cols = slice(g * group, (g + 1) * group)
        mean = sums[g][POOL_HALO:rows, cols] / _pool_counts(pos, width)
        o_ref[:, cols] = (mean - h[:, cols]).astype(BF16)
    ext[0:POOL_HALO, :] = h[tm - POOL_HALO:tm, :]

    @pl.when(t == pl.num_programs(1) - 1)
    def _():
        hist_ref[...] = h[tm - POOL_HIST:tm, :]


def _pool_prompt(x, mod, layer, tm):
    b, t, d = x.shape
    tok = pl.BlockSpec((None, tm, d), lambda bi, ti: (bi, ti, 0))
    return pl.pallas_call(
        functools.partial(_pool_prompt_kernel, tm=tm, group=d // len(POOL_WINDOWS)),
        grid=(b, t // tm),
        in_specs=[tok, _mod_spec(mod, layer, 0, tm), _mod_spec(mod, layer, 1, tm)],
        out_specs=[tok, pl.BlockSpec((None, POOL_HIST, d), lambda bi, ti: (bi, 0, 0))],
        out_shape=[jax.ShapeDtypeStruct((b, t, d), BF16), jax.ShapeDtypeStruct((b, POOL_HIST, d), F32)],
        scratch_shapes=[pltpu.VMEM((POOL_HALO + tm, d), F32)] * 3,
        compiler_params=_params("arbitrary", "arbitrary"),
    )(x, mod, mod)


def _pool_decode_kernel(x_ref, sh_ref, sc_ref, hist_ref, o_ref, hist_out_ref, *, group, pos0):
    h = x_ref[...] * (1.0 + sc_ref[...]) + sh_ref[...]
    rows = []
    for b in range(h.shape[0]):
        hb = h[b:b + 1, :]
        parts = []
        for g, width in enumerate(POOL_WINDOWS):
            cols = slice(g * group, (g + 1) * group)
            tail = hist_ref[b, POOL_HIST - (width - 1):POOL_HIST, cols]
            total = jnp.sum(tail, axis=0, keepdims=True) + hb[:, cols]
            parts.append(total / float(min(pos0 + 1, width)) - hb[:, cols])
        rows.append(jnp.concatenate(parts, axis=1))
        hist_out_ref[b, 0:POOL_HIST - 1, :] = hist_ref[b, 1:POOL_HIST, :]
        hist_out_ref[b, POOL_HIST - 1:POOL_HIST, :] = hb
    o_ref[...] = jnp.concatenate(rows, axis=0).astype(BF16)


def _pool_decode(x, mod, layer, hist, pos0):
    _, db, d = x.shape
    bb = 16
    assert db % bb == 0 and pos0 >= POOL_HIST
    tok = pl.BlockSpec((None, bb, d), lambda bi, ti: (0, ti, 0))
    hspec = pl.BlockSpec((bb, POOL_HIST, d), lambda bi, ti: (ti, 0, 0))
    mixed, hist_new = pl.pallas_call(
        functools.partial(_pool_decode_kernel, group=d // len(POOL_WINDOWS), pos0=pos0),
        grid=(1, db // bb),
        in_specs=[tok, _mod_spec(mod, layer, 0, bb), _mod_spec(mod, layer, 1, bb), hspec],
        out_specs=[tok, hspec],
        out_shape=[jax.ShapeDtypeStruct((1, db, d), BF16), jax.ShapeDtypeStruct((db, POOL_HIST, d), F32)],
        compiler_params=_params("arbitrary", "arbitrary"),
    )(x, mod, mod, hist)
    return mixed, hist_new


def _layer_norm(z, g, b):
    mu = jnp.mean(z, axis=-1, keepdims=True)
    zc = z - mu
    var = jnp.mean(zc * zc, axis=-1, keepdims=True)
    return zc * lax.rsqrt(var + LN_EPS) * g + b


def _first_max(vals):
    m = functools.reduce(jnp.maximum, vals)
    idx = jnp.full(m.shape, len(vals) - 1, I32)
    for i in reversed(range(len(vals) - 1)):
        idx = jnp.where(vals[i] == m, i, idx)
    return m, idx


def _top2(vals):
    m0, i0 = _first_max(vals)
    rest = [jnp.where(i0 == i, NEG_INF, v) for i, v in enumerate(vals)]
    m1, i1 = _first_max(rest)
    return m0, i0, m1, i1


def _post_kernel(o_ref, x_ref, g1_ref, sh2_ref, sc2_ref, w_ref, cs_ref, lng_ref, lnb_ref,
                 wr_hi_ref, wr_lo_ref, br_ref, xo_ref, h2_ref, rt_ref, *, alpha):
    out = _dot(o_ref[...], w_ref[...]) * cs_ref[...]
    xn = _layer_norm(alpha * x_ref[...] + g1_ref[...] * out, lng_ref[...], lnb_ref[...])
    xo_ref[...] = xn
    h2 = xn * (1.0 + sc2_ref[...]) + sh2_ref[...]
    h2_ref[...] = h2
    hi, lo = _split_bf16(h2)
    logits = _dot_nt(wr_hi_ref[...], hi) + _dot_nt(wr_hi_ref[...], lo) + _dot_nt(wr_lo_ref[...], hi) + br_ref[...]
    aff = jax.nn.sigmoid(logits)
    rows = [aff[e:e + 1, :] for e in range(N_EXPERTS)]
    gscores = []
    for g in range(N_EXPERT_GROUPS):
        m0, _, m1, _ = _top2(rows[g * EXPERTS_PER_GROUP:(g + 1) * EXPERTS_PER_GROUP])
        gscores.append(m0 + m1)
    _, g_sel = _first_max(gscores)
    in_grp = []
    for i in range(EXPERTS_PER_GROUP):
        v = rows[i]
        for g in range(1, N_EXPERT_GROUPS):
            v = jnp.where(g_sel == g, rows[g * EXPERTS_PER_GROUP + i], v)
        in_grp.append(v)
    w0, i0, w1, i1 = _top2(in_grp)
    denom = w0 + w1
    e0 = (g_sel * EXPERTS_PER_GROUP + i0).astype(F32)
    e1 = (g_sel * EXPERTS_PER_GROUP + i1).astype(F32)
    zero = jnp.zeros_like(w0)
    rt_ref[...] = jnp.concatenate([e0, e1, w0 / denom, w1 / denom, zero, zero, zero, zero], axis=0)


def _post_mixer(o, x, mod, layer, w_b, colscale, ln_g, ln_b, wr_hi, wr_lo, b_router, alpha, tm):
    b, t, d = x.shape
    tok = pl.BlockSpec((None, tm, d), lambda bi, ti: (bi, ti, 0))
    vec = pl.BlockSpec((1, d), lambda bi, ti: (0, 0))
    rtr = pl.BlockSpec((N_EXPERTS, d), lambda bi, ti: (0, 0))
    return pl.pallas_call(
        functools.partial(_post_kernel, alpha=alpha),
        grid=(b, t // tm),
        in_specs=[tok, tok, _mod_spec(mod, layer, 2, tm), _mod_spec(mod, layer, 3, tm), _mod_spec(mod, layer, 4, tm),
                  pl.BlockSpec((d, d), lambda bi, ti: (0, 0)), vec, vec, vec, rtr, rtr,
                  pl.BlockSpec((N_EXPERTS, 1), lambda bi, ti: (0, 0))],
        out_specs=[tok, tok, pl.BlockSpec((None, 8, tm), lambda bi, ti: (bi, 0, ti))],
        out_shape=[jax.ShapeDtypeStruct((b, t, d), F32), jax.ShapeDtypeStruct((b, t, d), F32),
                   jax.ShapeDtypeStruct((b, 8, t), F32)],
        compiler_params=_params("arbitrary", "arbitrary"),
    )(o, x, mod, mod, mod, w_b, colscale, ln_g, ln_b, wr_hi, wr_lo, b_router)


def _moe_kernel(be_ref, nact_ref, nval_ref, src_ref, src_next_ref, dst_ref, gate_ref, h_hbm, wg_ref, wu_ref, wd_ref,
                y_hbm, xg, yb, wgb, wub, wdb, sem_in, sem_out):
    i = pl.program_id(0)
    tb = xg.shape[1]
    n_active = nact_ref[0]
    slot = lax.rem(i, 2)

    def start_gather(idx_ref, to_slot):
        def body(g, carry):
            for u in range(DMA_UNROLL):
                r = g * DMA_UNROLL + u
                pltpu.make_async_copy(h_hbm.at[pl.ds(idx_ref[0, 0, r], 1), :], xg.at[to_slot, pl.ds(r, 1), :],
                                      sem_in.at[to_slot]).start()
            return carry
        lax.fori_loop(0, tb // DMA_UNROLL, body, 0)

    def scatter(r):
        return pltpu.make_async_copy(yb.at[pl.ds(r, 1), :], y_hbm.at[pl.ds(dst_ref[0, 0, r], 1), :], sem_out)

    def first_rows(n, fn):
        def group(g, carry):
            for u in range(DMA_UNROLL):
                fn(g * DMA_UNROLL + u)
            return carry
        groups = n // DMA_UNROLL
        lax.fori_loop(0, groups, group, 0)

        def single(r, carry):
            fn(r)
            return carry
        lax.fori_loop(groups * DMA_UNROLL, n, single, 0)

    @pl.when(i < n_active)
    def _():
        @pl.when(i == 0)
        def _():
            start_gather(src_ref, 0)

        changed = jnp.logical_or(i == 0, be_ref[i] != be_ref[jnp.maximum(i - 1, 0)])

        @pl.when(changed)
        def _():
            wgb[...] = wg_ref[...].astype(BF16)
            wub[...] = wu_ref[...].astype(BF16)
            wdb[...] = wd_ref[...].astype(BF16)

        pltpu.make_async_copy(h_hbm.at[pl.ds(0, tb), :], xg.at[slot], sem_in.at[slot]).wait()

        @pl.when(i > 0)
        def _():
            first_rows(nval_ref[jnp.maximum(i - 1, 0)], lambda r: scatter(r).wait())

        def expert(cur):
            x = xg[cur].astype(BF16)
            a = _dot(x, wgb[...])
            a = a * jax.nn.sigmoid(a) * _dot(x, wub[...])
            yb[...] = _dot(a.astype(BF16), wdb[...]) * gate_ref[...]

        for cur in range(2):
            @pl.when(jnp.logical_and(slot == cur, i + 1 < n_active))
            def _(cur=cur):
                for r in range(tb):
                    pltpu.make_async_copy(h_hbm.at[pl.ds(src_next_ref[0, 0, r], 1), :],
                                          xg.at[1 - cur, pl.ds(r, 1), :], sem_in.at[1 - cur]).start()
                expert(cur)

            @pl.when(jnp.logical_and(slot == cur, i + 1 >= n_active))
            def _(cur=cur):
                expert(cur)

        first_rows(nval_ref[i], lambda r: scatter(r).start())

        @pl.when(i == n_active - 1)
        def _():
            first_rows(nval_ref[i], lambda r: scatter(r).wait())


def _moe_experts(h2_flat, blk_expert, n_active, n_valid, src_tok, dst_row, gate_rows, w_gate, w_up, w_down, layer,
                 n_out_rows):
    n_rows = src_tok.shape[0]
    tb = MOE_ROWS
    n_blk = n_rows // tb
    d = h2_flat.shape[1]
    de = w_gate.shape[3]
    idx = pl.BlockSpec((1, 1, tb), lambda i, *_: (i, 0, 0), memory_space=pltpu.SMEM)
    idx_next = pl.BlockSpec((1, 1, tb), lambda i, *_: (jnp.minimum(i + 1, n_blk - 1), 0, 0), memory_space=pltpu.SMEM)
    src3 = src_tok.reshape(n_blk, 1, tb)
    return pl.pallas_call(
        _moe_kernel,
        grid_spec=pltpu.PrefetchScalarGridSpec(
            num_scalar_prefetch=3,
            grid=(n_blk,),
            in_specs=[
                idx, idx_next, idx,
                pl.BlockSpec((tb, 1), lambda i, *_: (i, 0)),
                pl.BlockSpec(memory_space=pl.ANY),
                pl.BlockSpec((None, None, d, de), lambda i, be, *_: (layer, be[i], 0, 0)),
                pl.BlockSpec((None, None, d, de), lambda i, be, *_: (layer, be[i], 0, 0)),
                pl.BlockSpec((None, None, de, d), lambda i, be, *_: (layer, be[i], 0, 0)),
            ],
            out_specs=pl.BlockSpec(memory_space=pl.ANY),
            scratch_shapes=[
                pltpu.VMEM((2, tb, d), F32), pltpu.VMEM((tb, d), F32),
                pltpu.VMEM((d, de), BF16), pltpu.VMEM((d, de), BF16), pltpu.VMEM((de, d), BF16),
                pltpu.SemaphoreType.DMA((2,)), pltpu.SemaphoreType.DMA(()),
            ],
        ),
        out_shape=jax.ShapeDtypeStruct((n_out_rows, d), F32),
        compiler_params=_params("arbitrary"),
    )(blk_expert, n_active, n_valid, src3, src3, dst_row.reshape(n_blk, 1, tb),
      gate_rows.reshape(n_rows, 1), h2_flat, w_gate, w_up, w_down)


def _route_tables(rt, n_tok):
    tb = MOE_ROWS
    m = 2 * n_tok
    e = jnp.stack([rt[:, 0, :], rt[:, 1, :]], axis=-1).reshape(m).astype(I32)
    g = jnp.stack([rt[:, 2, :], rt[:, 3, :]], axis=-1).reshape(m)
    onehot = (e[:, None] == jnp.arange(N_EXPERTS, dtype=I32)[None, :]).astype(I32)
    counts = onehot.sum(0)
    rank = jnp.sum((jnp.cumsum(onehot, axis=0) - onehot) * onehot, axis=1)
    padded = (counts + tb - 1) // tb * tb
    pad_ends = jnp.cumsum(padded)
    pad_starts = pad_ends - padded
    dest = pad_starts[e] + rank
    n_blk = -(-(m + N_EXPERTS * (tb - 1)) // tb)
    n_rows = n_blk * tb
    pair = jnp.arange(m, dtype=I32)
    packed = jnp.stack([pair // 2, (pair % 2) * n_tok + pair // 2, lax.bitcast_convert_type(g, I32)], axis=1)
    table = jnp.zeros((n_rows, 3), I32).at[dest].set(packed)
    src_tok, dst_row, gate_rows = table[:, 0], table[:, 1], lax.bitcast_convert_type(table[:, 2], F32)
    blk_start = jnp.arange(n_blk, dtype=I32) * tb
    blk_expert = jnp.minimum(jnp.sum(blk_start[:, None] >= pad_ends[None, :], axis=1), N_EXPERTS - 1).astype(I32)
    n_active = (pad_ends[-1] // tb).astype(I32).reshape(1)
    n_valid = jnp.clip(counts[blk_expert] - (blk_start - pad_starts[blk_expert]), 0, tb).astype(I32)
    return blk_expert, n_active, n_valid, src_tok, dst_row, gate_rows, m


def _final_kernel(x_ref, y0_ref, y1_ref, g2_ref, lng_ref, lnb_ref, o_ref, *, alpha):
    ffn = y0_ref[...] + y1_ref[...]
    o_ref[...] = _layer_norm(alpha * x_ref[...] + g2_ref[...] * ffn, lng_ref[...], lnb_ref[...])


def _final_ln(x, y_slots, mod, layer, ln_g, ln_b, alpha, tm):
    b, t, d = x.shape
    tok = pl.BlockSpec((None, tm, d), lambda bi, ti: (bi, ti, 0))
    vec = pl.BlockSpec((1, d), lambda bi, ti: (0, 0))
    tiles = t // tm

    def slot(k):
        return pl.BlockSpec((None, tm, d), lambda bi, ti: (k, bi * tiles + ti, 0))

    return pl.pallas_call(
        functools.partial(_final_kernel, alpha=alpha),
        grid=(b, tiles),
        in_specs=[tok, slot(0), slot(1), _mod_spec(mod, layer, 5, tm), vec, vec],
        out_specs=tok,
        out_shape=jax.ShapeDtypeStruct((b, t, d), F32),
        compiler_params=_params("arbitrary", "arbitrary"),
    )(x, y_slots, y_slots, mod, ln_g, ln_b)


def _run_trunk(x, mod, pos0, cache_k, cache_v, page_table, pool_state, prm, tm):
    b, t, d = x.shape
    depth = prm["depth"]
    alpha = (2.0 * depth) ** 0.25
    decode = cache_k is not None
    new_k, new_v, new_pool = [], [], []
    for i in range(depth):
        j = i // 2
        if i % 2 == 0:
            if decode:
                k, v, q = _qkv(x, mod, i, prm["w_qkv_b"][j], tm, prompt=False)
                per_head = (t, N_HEADS, HEAD_DIM)
                o = _moba_decode(q.astype(F32).reshape(per_head), k.reshape(per_head), v.reshape(per_head),
                                 cache_k, cache_v, j, page_table, prm["bias_decode"])
                o = o.reshape(1, t, d).astype(BF16)
            else:
                k, v, qt, vt, kh, km = _qkv(x, mod, i, prm["w_qkv_b"][j], tm, prompt=True)
                o = _moba_prompt(qt, kh, vt, km.transpose(0, 2, 1, 3), prm["slope_rows"], prm["block_steps"])
            new_k.append(k)
            new_v.append(v)
            w_b, colscale = prm["w_o_b"][j], prm["ones"]
        else:
            if decode:
                o, hist_new = _pool_decode(x, mod, i, pool_state[j], pos0)
            else:
                o, hist_new = _pool_prompt(x, mod, i, tm)
            new_pool.append(hist_new)
            w_b, colscale = prm["w_pool_b"][j], prm["pool_scale"][j]
        x, h2, rt = _post_mixer(o, x, mod, i, w_b, colscale, prm["ln_g"][i, 0], prm["ln_b"][i, 0],
                                prm["wr_hi"], prm["wr_lo"], prm["b_router"], alpha, tm)
        n_tok = b * t
        blk_expert, n_active, n_valid, src_tok, dst_row, gate_rows, n_out_rows = _route_tables(rt, n_tok)
        y = _moe_experts(h2.reshape(n_tok, d), blk_expert, n_active, n_valid, src_tok, dst_row, gate_rows,
                         prm["w_gate"], prm["w_up"], prm["w_down"], i, n_out_rows)
        x = _final_ln(x, y.reshape(2, n_tok, d), mod, i, prm["ln_g"][i, 1], prm["ln_b"][i, 1], alpha, tm)
    return x, new_k, new_v, new_pool


def kernel(x_prompt, x_sample, cache_k, cache_v, state_pool, page_table, c_prompt, c_sample, w_ada, b_ada, ln_g, ln_b, w_qkv, w_o, w_pool, pool_scale, w_router, b_router, w_gate, w_up, w_down):
    bp, seq, d = x_prompt.shape
    db, dec_seq, _ = x_sample.shape
    depth = w_ada.shape[0]
    assert dec_seq == 1 and d == N_HEADS * HEAD_DIM and seq % MOBA_BLOCK == 0
    pos0 = page_table.shape[1] * cache_k.shape[2]

    c_all = jnp.concatenate([c_prompt, c_sample], axis=0)
    pad = (-c_all.shape[0]) % 8
    c_all = jnp.pad(c_all, ((0, pad), (0, 0)))
    mod = _adaln(c_all, w_ada, b_ada).reshape(depth, c_all.shape[0], 6, d).transpose(0, 2, 1, 3)
    mod_p = mod[:, :, :bp].reshape(depth, 6, bp, 1, d)
    mod_s = mod[:, :, bp:bp + db].reshape(depth, 6, 1, db, d)

    group = d // len(POOL_WINDOWS)
    w_pool_dense = jnp.zeros((w_pool.shape[0], d, d), F32)
    for g in range(len(POOL_WINDOWS)):
        w_pool_dense = w_pool_dense.at[:, g * group:(g + 1) * group, g * group:(g + 1) * group].set(w_pool[:, g])
    wr_hi, wr_lo = _split_bf16(w_router.T)
    slopes2 = LOG2E * 2.0 ** (-(8.0 / N_HEADS) * jnp.arange(1, N_HEADS + 1, dtype=F32))
    c_hi, c_lo = _split_bf16(slopes2)
    slope_rows = jnp.zeros((N_HEADS, LANES - HEAD_DIM, MOBA_BLOCK), BF16)
    slope_rows = slope_rows.at[:, 0, :].set(c_hi[:, None]).at[:, 1, :].set(c_lo[:, None])
    bias_decode = -slopes2[:, None] * (pos0 - jnp.arange(pos0, dtype=I32)).astype(F32)[None, :]
    cache_kt = cache_k.transpose(0, 1, 3, 4, 2)
    cache_vt = cache_v.transpose(0, 1, 3, 4, 2)
    prm = dict(
        depth=depth,
        w_qkv_b=w_qkv.astype(BF16), w_o_b=w_o.astype(BF16), w_pool_b=w_pool_dense.astype(BF16),
        pool_scale=pool_scale.reshape(-1, 1, d), ones=jnp.ones((1, d), F32),
        ln_g=ln_g.reshape(depth, 2, 1, d), ln_b=ln_b.reshape(depth, 2, 1, d),
        wr_hi=wr_hi, wr_lo=wr_lo, b_router=b_router.reshape(N_EXPERTS, 1),
        w_gate=w_gate, w_up=w_up, w_down=w_down,
        slope_rows=slope_rows, block_steps=slopes2 * MOBA_BLOCK, bias_decode=bias_decode,
    )

    y_p, k_p, v_p, pool_p = _run_trunk(x_prompt, mod_p, 0, None, None, None, None, prm, tm=MOBA_BLOCK)
    y_s, k_s, v_s, pool_s = _run_trunk(
        x_sample.reshape(1, db, d), mod_s, pos0,
        cache_kt, cache_vt, page_table, state_pool, prm, tm=db)

    def heads(ts, lead):
        return jnp.stack(ts).reshape(len(ts), *lead, N_HEADS, HEAD_DIM)

    return (y_p, y_s.reshape(db, 1, d),
            heads(k_p, (bp, seq)), heads(v_p, (bp, seq)), jnp.stack(pool_p),
            heads(k_s, (db, 1)), heads(v_s, (db, 1)), jnp.stack(pool_s))
```
